```python
import math
import jax, jax.numpy as jnp
from jax import lax
import numpy as np

D_MODEL = 1024
BATCH = 16
SEQ = 2048
DEPTH = 2

CHUNK = 64
Q_BLOCK = 128
N_MIXERS = 2
N_A_LAYERS = (DEPTH + 1) // 2
N_B_LAYERS = DEPTH // 2
FOX_HEADS = 16
FOX_HEAD_DIM = D_MODEL // FOX_HEADS
GLA_HEADS = 4
GLA_DK = D_MODEL // 2
GLA_DV = D_MODEL
GLA_HK = GLA_DK // GLA_HEADS
GLA_HV = GLA_DV // GLA_HEADS
GLA_RANK = 16
GLA_TAU = 16.0
FFN_HIDDEN = int(math.ceil(8 * D_MODEL / 3 / 256) * 256)
N_MOD = 6
EPS = 1e-6

kernel_name = "fox_gla_adaln_hybrid_trunk"


def rms_norm(x, g):
    xf = x.astype(jnp.float32)
    y = xf * lax.rsqrt(jnp.mean(xf * xf, axis=-1, keepdims=True) + EPS)
    return (y * g.astype(jnp.float32)).astype(x.dtype)


def modulate(h, shift, scale):
    return h * (1.0 + scale[:, None, :]) + shift[:, None, :]


def forgetting_attention(h, w_in, b_f, w_out):
    B, S, _ = h.shape
    proj = h @ w_in
    q, k, v, f_logit = jnp.split(proj, [D_MODEL, 2 * D_MODEL, 3 * D_MODEL], axis=-1)
    q = q.reshape(B, S, FOX_HEADS, FOX_HEAD_DIM)
    k = k.reshape(B, S, FOX_HEADS, FOX_HEAD_DIM)
    v = v.reshape(B, S, FOX_HEADS, FOX_HEAD_DIM)
    log_f = jax.nn.log_sigmoid(f_logit.astype(jnp.float32) + b_f.astype(jnp.float32))
    cum = jnp.cumsum(log_f, axis=1).transpose(0, 2, 1)
    scale = FOX_HEAD_DIM ** -0.5
    outs = []
    for blk in range(S // Q_BLOCK):
        qs, qe = blk * Q_BLOCK, (blk + 1) * Q_BLOCK
        logits = jnp.einsum('bqhd,bkhd->bhqk', q[:, qs:qe], k[:, :qe]).astype(jnp.float32) * scale
        logits = logits + cum[:, :, qs:qe, None] - cum[:, :, None, :qe]
        mask = (qs + jnp.arange(Q_BLOCK))[:, None] >= jnp.arange(qe)[None, :]
        logits = jnp.where(mask[None, None], logits, -jnp.inf)
        p = jax.nn.softmax(logits, axis=-1).astype(v.dtype)
        outs.append(jnp.einsum('bhqk,bkhd->bqhd', p, v[:, :qe]))
    o = jnp.concatenate(outs, axis=1).reshape(B, S, D_MODEL)
    return o @ w_out


def gated_linear_attention(h, w_in, w_a2, b_a, g_o, w_out):
    B, S, _ = h.shape
    N = S // CHUNK
    proj = h @ w_in
    q, k, v, r, a_lr = jnp.split(
        proj, [GLA_DK, 2 * GLA_DK, 2 * GLA_DK + GLA_DV, 2 * GLA_DK + 2 * GLA_DV], axis=-1)
    log_alpha = jax.nn.log_sigmoid((a_lr @ w_a2 + b_a).astype(jnp.float32)) / GLA_TAU
    q = (q.astype(jnp.float32) * GLA_HK ** -0.5).reshape(B, N, CHUNK, GLA_HEADS, GLA_HK)
    k = k.astype(jnp.float32).reshape(B, N, CHUNK, GLA_HEADS, GLA_HK)
    v = v.astype(jnp.float32).reshape(B, N, CHUNK, GLA_HEADS, GLA_HV)
    b = jnp.cumsum(log_alpha.reshape(B, N, CHUNK, GLA_HEADS, GLA_HK), axis=2)
    b_last = b[:, :, -1]
    q_in = q * jnp.exp(b)
    k_in = k * jnp.exp(-b)
    A = jnp.einsum('bnthk,bnshk->bnhts', q_in, k_in)
    tril = jnp.tril(jnp.ones((CHUNK, CHUNK), dtype=bool))
    A = jnp.where(tril, A, 0.0)
    o_intra = jnp.einsum('bnhts,bnshv->bnthv', A, v)
    k_dec = k * jnp.exp(b_last[:, :, None] - b)
    kv = jnp.einsum('bnshk,bnshv->bnhkv', k_dec, v)

    def step(state, xs):
        dec, kv_n = xs
        return dec[..., None] * state + kv_n, state

    state0 = jnp.zeros((B, GLA_HEADS, GLA_HK, GLA_HV), jnp.float32)
    _, s_prev = lax.scan(step, state0, (jnp.moveaxis(jnp.exp(b_last), 1, 0), jnp.moveaxis(kv, 1, 0)))
    o_inter = jnp.einsum('bnthk,nbhkv->bnthv', q_in, s_prev)
    o = (o_intra + o_inter).reshape(B, S, GLA_HEADS, GLA_HV)
    o = o * lax.rsqrt(jnp.mean(o * o, axis=-1, keepdims=True) + EPS)
    o = o.reshape(B, S, GLA_DV) * g_o.astype(jnp.float32)
    o = (o * jax.nn.silu(r.astype(jnp.float32))).astype(h.dtype)
    return o @ w_out


def swiglu(h, w_in, w_out):
    gate, up = jnp.split(h @ w_in, 2, axis=-1)
    return (jax.nn.silu(gate) * up) @ w_out


def setup_inputs(seed: int = 0) -> dict:
    key = jax.random.key(seed)
    ks = jax.random.split(key, 20)
    f32 = jnp.float32

    def nrm(k, shape, fan_in, mult=1.0):
        return jax.random.normal(k, shape, f32) * (mult * fan_in ** -0.5)

    D = D_MODEL
    fox_cols = 3 * D + FOX_HEADS
    gla_cols = 2 * GLA_DK + 2 * GLA_DV + GLA_RANK
    return {
        "x": jax.random.normal(ks[0], (BATCH, SEQ, D), f32),
        "c": jax.random.normal(ks[1], (BATCH, D), f32),
        "ada_w": nrm(ks[2], (DEPTH, D, N_MOD * D), D, 0.5),
        "ada_b": 0.01 * jax.random.normal(ks[3], (DEPTH, N_MOD * D), f32),
        "norm1_g": 1.0 + 0.02 * jax.random.normal(ks[4], (DEPTH, D), f32),
        "norm2_g": 1.0 + 0.02 * jax.random.normal(ks[5], (DEPTH, D), f32),
        "ffn_w_in": nrm(ks[6], (DEPTH, D, 2 * FFN_HIDDEN), D),
        "ffn_w_out": nrm(ks[7], (DEPTH, FFN_HIDDEN, D), FFN_HIDDEN),
        "fox_w_in": nrm(ks[8], (N_A_LAYERS, D, fox_cols), D),
        "fox_b_f": jax.random.uniform(ks[9], (N_A_LAYERS, FOX_HEADS), f32, 1.0, 4.0),
        "fox_w_out": nrm(ks[10], (N_A_LAYERS, D, D), D),
        "gla_w_in": nrm(ks[11], (N_B_LAYERS, D, gla_cols), D),
        "gla_w_a2": nrm(ks[12], (N_B_LAYERS, GLA_RANK, GLA_DK), GLA_RANK),
        "gla_b_a": 0.1 * jax.random.normal(ks[13], (N_B_LAYERS, GLA_DK), f32),
        "gla_g_o": 1.0 + 0.02 * jax.random.normal(ks[14], (N_B_LAYERS, GLA_DV), f32),
        "gla_w_out": nrm(ks[15], (N_B_LAYERS, GLA_DV, D), GLA_DV),
        "final_g": 1.0 + 0.02 * jax.random.normal(ks[16], (D,), f32),
    }


def reference(x, c, ada_w, ada_b, norm1_g, norm2_g, ffn_w_in, ffn_w_out,
              fox_w_in, fox_b_f, fox_w_out,
              gla_w_in, gla_w_a2, gla_b_a, gla_g_o, gla_w_out, final_g):
    c_act = jax.nn.silu(c)
    for i in range(DEPTH):
        mod = c_act @ ada_w[i] + ada_b[i]
        sh1, sc1, g1, sh2, sc2, g2 = jnp.split(mod, N_MOD, axis=-1)
        h = modulate(rms_norm(x, norm1_g[i]), sh1, sc1)
        j = i // N_MIXERS
        if i % N_MIXERS == 0:
            y = forgetting_attention(h, fox_w_in[j], fox_b_f[j], fox_w_out[j])
        else:
            y = gated_linear_attention(h, gla_w_in[j], gla_w_a2[j], gla_b_a[j], gla_g_o[j], gla_w_out[j])
        x = x + g1[:, None, :] * y
        h = modulate(rms_norm(x, norm2_g[i]), sh2, sc2)
        x = x + g2[:, None, :] * swiglu(h, ffn_w_in[i], ffn_w_out[i])
    return rms_norm(x, final_g)
```

```python
import functools

import jax
import jax.numpy as jnp
from jax import lax
from jax.experimental import pallas as pl
from jax.experimental.pallas import tpu as pltpu

F32 = jnp.float32
BF16 = jnp.bfloat16

EPS = 1e-6
N_MOD = 6
CHUNK = 64
FOX_HEADS = 16
GLA_HEADS = 4
GLA_TAU = 16.0
LANES = 128
BIAS_LANES = 8
N_SPLIT = 3

TOKEN_TILE = 512
ATTN_TILE = 256
FFN_CHUNK = 256
CUM_BLOCK = 128
VMEM_LIMIT = 56 * 1024 * 1024


def _params(n_grid):
    return pltpu.CompilerParams(
        dimension_semantics=("arbitrary",) * n_grid, vmem_limit_bytes=VMEM_LIMIT)


def _resident(shape):
    return pl.BlockSpec(shape, lambda *_: (0,) * len(shape), pipeline_mode=pl.Buffered(1))


def _rms_mod(x, g, shift, scale):
    y = x * lax.rsqrt(jnp.mean(x * x, axis=-1, keepdims=True) + EPS) * g
    return y * (1.0 + scale) + shift


def _log_sigmoid(x):
    return jnp.minimum(x, 0.0) - jnp.log1p(jnp.exp(-jnp.abs(x)))


def _silu(x):
    return x * jax.nn.sigmoid(x)


def _split_bf16(x):
    terms = []
    for _ in range(N_SPLIT - 1):
        t = x.astype(BF16)
        terms.append(t)
        x = x - t.astype(F32)
    terms.append(x.astype(BF16))
    return terms


def _dot(a, b):
    return jnp.dot(a, b, preferred_element_type=F32)


def _dot_nt(a, b):
    return lax.dot_general(a, b, (((1,), (1,)), ((), ())), preferred_element_type=F32)


def _dot_tn(a, b):
    return lax.dot_general(a, b, (((0,), (0,)), ((), ())), preferred_element_type=F32)


def _adaln_kernel(c_ref, w_ref, b_ref, o_ref):
    ca = _silu(c_ref[...]).astype(BF16)
    o_ref[...] = _dot(ca, w_ref[...].astype(BF16)) + b_ref[...]


def _adaln_mod(c, ada_w, ada_b):
    depth, d, n = ada_w.shape
    b = c.shape[0]
    bn = 1024
    return pl.pallas_call(
        _adaln_kernel,
        grid=(depth, n // bn),
        in_specs=[
            pl.BlockSpec((b, d), lambda i, j: (0, 0)),
            pl.BlockSpec((None, d, bn), lambda i, j: (i, 0, j)),
            pl.BlockSpec((None, 1, bn), lambda i, j: (i, 0, j)),
        ],
        out_specs=pl.BlockSpec((None, b, bn), lambda i, j: (i, 0, j)),
        out_shape=jax.ShapeDtypeStruct((depth, b, n), F32),
        compiler_params=_params(2),
        name="adaln_mod",
    )(c, ada_w, ada_b.reshape(depth, 1, n))


def _fox_proj_kernel(x_ref, mod_ref, g_ref, w_ref, bf_ref,
                     q_ref, k_ref, v_ref, qx_ref, kx_ref, carry_ref, *, d, scale):
    tm = x_ref.shape[0]

    @pl.when(pl.program_id(1) == 0)
    def _():
        carry_ref[...] = jnp.zeros_like(carry_ref)

    h = _rms_mod(x_ref[...], g_ref[...], mod_ref[0:1, :], mod_ref[1:2, :]).astype(BF16)
    proj = _dot(h, w_ref[...])
    q_ref[...] = (proj[:, :d] * scale).astype(BF16)
    k_ref[...] = proj[:, d:2 * d].astype(BF16)
    v_ref[...] = proj[:, 2 * d:3 * d].astype(BF16)

    lane = lax.broadcasted_iota(jnp.int32, (1, LANES), 1)
    used = lane < N_SPLIT * FOX_HEADS
    log_f = jnp.where(used, _log_sigmoid(proj[:, 3 * d:] + bf_ref[...]), 0.0)
    row = lax.broadcasted_iota(jnp.int32, (tm, tm), 0)
    col = lax.broadcasted_iota(jnp.int32, (tm, tm), 1)
    tril = jnp.where(row >= col, 1.0, 0.0).astype(BF16)
    sums = _dot(tril, jnp.concatenate(_split_bf16(log_f), axis=1))
    cum = carry_ref[...] + sum(sums[:, t * LANES:(t + 1) * LANES] for t in range(N_SPLIT))
    carry_ref[...] = cum[tm - 1:tm, :]

    terms = _split_bf16(cum)
    group = lane // FOX_HEADS
    packed = jnp.zeros_like(terms[0])
    for t in range(N_SPLIT):
        packed = jnp.where(group == t, terms[t], packed)
    src = lax.broadcasted_iota(jnp.int32, (LANES, LANES), 0)
    dst = lax.broadcasted_iota(jnp.int32, (LANES, LANES), 1)
    same_head = (src % FOX_HEADS == dst // BIAS_LANES) & (src < N_SPLIT * FOX_HEADS)
    place_q = jnp.where(same_head & (src // FOX_HEADS == dst % BIAS_LANES), 1.0, 0.0)
    place_k = jnp.where(same_head & (src // FOX_HEADS + N_SPLIT == dst % BIAS_LANES), 1.0, 0.0)
    pos = lane % BIAS_LANES
    ones_q = jnp.where((pos >= N_SPLIT) & (pos < 2 * N_SPLIT), 1.0, 0.0)
    ones_k = jnp.where(pos < N_SPLIT, 1.0, 0.0)
    qx_ref[...] = (_dot(packed, place_q.astype(BF16)) + ones_q).astype(BF16)
    kx_ref[...] = (ones_k - _dot(packed, place_k.astype(BF16))).astype(BF16)


def _fox_proj(x, mod, g, w, b_f, layer):
    b, s, d = x.shape
    tm = TOKEN_TILE
    tok = lambda width: pl.BlockSpec((None, tm, width), lambda bi, si: (bi, si, 0))
    out_bf16 = lambda width: jax.ShapeDtypeStruct((b, s, width), BF16)
    return pl.pallas_call(
        functools.partial(_fox_proj_kernel, d=d, scale=(d // FOX_HEADS) ** -0.5),
        grid=(b, s // tm),
        in_specs=[
            tok(d),
            pl.BlockSpec((None, None, N_MOD, d), lambda bi, si: (layer, bi, 0, 0)),
            pl.BlockSpec((None, 1, d), lambda bi, si: (layer, 0, 0)),
            _resident(w.shape),
            _resident(b_f.shape),
        ],
        out_specs=[tok(d), tok(d), tok(d), tok(LANES), tok(LANES)],
        out_shape=[out_bf16(d), out_bf16(d), out_bf16(d), out_bf16(LANES), out_bf16(LANES)],
        scratch_shapes=[pltpu.VMEM((1, LANES), F32)],
        compiler_params=_params(2),
        name="fox_proj",
    )(x, mod, g, w, b_f)


def _fox_attn_kernel(q_ref, qx_ref, k_ref, kx_ref, v_ref, o_ref, acc_ref):
    tq = q_ref.shape[0]
    half = LANES // 2
    pair = pl.program_id(1)
    qi = pl.program_id(2)
    lane = lax.broadcasted_iota(jnp.int32, (1, 2 * LANES), 1)
    q_full = jnp.concatenate([q_ref[...], qx_ref[...]], axis=1)
    ones = jnp.ones((tq, LANES), BF16)
    row = lax.broadcasted_iota(jnp.int32, (tq, tq), 0)
    col = lax.broadcasted_iota(jnp.int32, (tq, tq), 1)
    causal = row >= col

    def kv_block(j):
        rows = pl.ds(pl.multiple_of(j * tq, tq), tq)
        kb = jnp.concatenate([k_ref[rows, :], kx_ref[rows, :]], axis=1)
        vb = jnp.concatenate([v_ref[rows, :], ones], axis=1)
        return kb, vb

    outs = []
    for hh in range(2):
        bias_lo = LANES + BIAS_LANES * (2 * pair + hh)
        keep = (((lane >= hh * half) & (lane < (hh + 1) * half))
                | ((lane >= bias_lo) & (lane < bias_lo + BIAS_LANES)))
        qh = jnp.where(keep, q_full, jnp.zeros_like(q_full))

        kb, vb = kv_block(qi)
        s = jnp.where(causal, _dot_nt(qh, kb), -jnp.inf)
        m0 = jnp.max(s, axis=1, keepdims=True)
        acc_ref[hh] = _dot(jnp.exp(s - m0).astype(BF16), vb)

        def body(j, m, qh=qh, hh=hh):
            kb, vb = kv_block(j)
            s = _dot_nt(qh, kb)
            m_new = jnp.maximum(m, jnp.max(s, axis=1, keepdims=True))
            p = jnp.exp(s - m_new).astype(BF16)
            acc_ref[hh] = jnp.exp(m - m_new) * acc_ref[hh] + _dot(p, vb)
            return m_new

        lax.fori_loop(0, qi, body, m0)
        acc = acc_ref[hh]
        outs.append(acc[:, :LANES] / acc[:, LANES:])

    lane_o = lax.broadcasted_iota(jnp.int32, (1, LANES), 1)
    o_ref[...] = jnp.where(lane_o < half, outs[0], outs[1]).astype(BF16)


def _fox_attn(q, qx, k, kx, v):
    b, s, d = q.shape
    tq = ATTN_TILE
    q_blk = lambda col: pl.BlockSpec((None, tq, LANES), col)
    kv_blk = lambda col: pl.BlockSpec((None, s, LANES), col)
    return pl.pallas_call(
        _fox_attn_kernel,
        grid=(b, d // LANES, s // tq),
        in_specs=[
            q_blk(lambda bi, p, qi: (bi, qi, p)),
            q_blk(lambda bi, p, qi: (bi, qi, 0)),
            kv_blk(lambda bi, p, qi: (bi, 0, p)),
            kv_blk(lambda bi, p, qi: (bi, 0, 0)),
            kv_blk(lambda bi, p, qi: (bi, 0, p)),
        ],
        out_specs=q_blk(lambda bi, p, qi: (bi, qi, p)),
        out_shape=jax.ShapeDtypeStruct((b, s, d), BF16),
        scratch_shapes=[pltpu.VMEM((2, tq, 2 * LANES), F32)],
        compiler_params=_params(3),
        name="fox_attn",
    )(q, qx, k, kx, v)


def _gla_proj_kernel(x_ref, mod_ref, g_ref, w_ref, wa_ref, wa2_ref, ba_ref,
                     qin_ref, kin_ref, kdec_ref, v_ref, r_ref, dec_ref, b_ref, *, dk, dv, scale):
    tm = x_ref.shape[0]
    h = _rms_mod(x_ref[...], g_ref[...], mod_ref[0:1, :], mod_ref[1:2, :]).astype(BF16)
    proj = _dot(h, w_ref[...])
    v_ref[...] = proj[:, 2 * dk:2 * dk + dv].astype(BF16)
    r_ref[...] = proj[:, 2 * dk + dv:]
    a_lr = _dot(h, wa_ref[...]).astype(BF16)
    log_alpha = _log_sigmoid(_dot(a_lr, wa2_ref[...]) + ba_ref[...]) / GLA_TAU

    row = lax.broadcasted_iota(jnp.int32, (CUM_BLOCK, CUM_BLOCK), 0)
    col = lax.broadcasted_iota(jnp.int32, (CUM_BLOCK, CUM_BLOCK), 1)
    tril = jnp.where((row >= col) & (row // CHUNK == col // CHUNK), 1.0, 0.0).astype(BF16)
    terms = jnp.concatenate(_split_bf16(log_alpha), axis=1)
    for i in range(tm // CUM_BLOCK):
        rows = slice(i * CUM_BLOCK, (i + 1) * CUM_BLOCK)
        sums = _dot(tril, terms[rows, :])
        b_ref[rows, :] = sum(sums[:, t * dk:(t + 1) * dk] for t in range(N_SPLIT))

    for c in range(tm // CHUNK):
        rows = slice(c * CHUNK, (c + 1) * CHUNK)
        b = b_ref[rows, :]
        b_last = b_ref[(c + 1) * CHUNK - 1:(c + 1) * CHUNK, :]
        k = proj[rows, dk:2 * dk]
        qin_ref[rows, :] = (proj[rows, :dk] * scale * jnp.exp(b)).astype(BF16)
        kin_ref[rows, :] = (k * jnp.exp(-b)).astype(BF16)
        kdec_ref[rows, :] = (k * jnp.exp(b_last - b)).astype(BF16)
        dec_ref[c:c + 1, :] = jnp.exp(b_last)


def _gla_proj(x, mod, g, w, wa, wa2, ba, layer):
    b, s, d = x.shape
    tm = TOKEN_TILE
    dk = wa2.shape[1]
    dv = (w.shape[1] - 2 * dk) // 2
    tok = lambda width: pl.BlockSpec((None, tm, width), lambda bi, si: (bi, si, 0))
    out = lambda width, dt: jax.ShapeDtypeStruct((b, s, width), dt)
    return pl.pallas_call(
        functools.partial(_gla_proj_kernel, dk=dk, dv=dv, scale=(dk // GLA_HEADS) ** -0.5),
        grid=(b, s // tm),
        in_specs=[
            tok(d),
            pl.BlockSpec((None, None, N_MOD, d), lambda bi, si: (layer, bi, 0, 0)),
            pl.BlockSpec((None, 1, d), lambda bi, si: (layer, 0, 0)),
            _resident(w.shape),
            _resident(wa.shape),
            _resident(wa2.shape),
            _resident(ba.shape),
        ],
        out_specs=[tok(dk), tok(dk), tok(dk), tok(dv), tok(dv),
                   pl.BlockSpec((None, tm // CHUNK, dk), lambda bi, si: (bi, si, 0))],
        out_shape=[out(dk, BF16), out(dk, BF16), out(dk, BF16), out(dv, BF16), out(dv, F32),
                   jax.ShapeDtypeStruct((b, s // CHUNK, dk), F32)],
        scratch_shapes=[pltpu.VMEM((tm, dk), F32)],
        compiler_params=_params(2),
        name="gla_proj",
    )(x, mod, g, w, wa, wa2, ba)


def _gla_core_kernel(qin_ref, kin_ref, kdec_ref, v_ref, r_ref, dec_ref, go_ref,
                     o_ref, state_ref):
    tm = qin_ref.shape[0]
    hk = qin_ref.shape[1] // GLA_HEADS
    hv = v_ref.shape[1] // GLA_HEADS

    @pl.when(pl.program_id(1) == 0)
    def _():
        state_ref[...] = jnp.zeros_like(state_ref)

    row = lax.broadcasted_iota(jnp.int32, (CHUNK, CHUNK), 0)
    col = lax.broadcasted_iota(jnp.int32, (CHUNK, CHUNK), 1)
    causal = row >= col

    for c in range(tm // CHUNK):
        rows = slice(c * CHUNK, (c + 1) * CHUNK)
        for hd in range(GLA_HEADS):
            kc = slice(hd * hk, (hd + 1) * hk)
            vc = slice(hd * hv, (hd + 1) * hv)
            q = qin_ref[rows, kc]
            v = v_ref[rows, vc]
            a = jnp.where(causal, _dot_nt(q, kin_ref[rows, kc]), 0.0).astype(BF16)
            state_t = state_ref[hd]
            o = _dot(a, v) + _dot_nt(q, state_t.astype(BF16))
            state_ref[hd] = state_t * dec_ref[c:c + 1, kc] + _dot_tn(v, kdec_ref[rows, kc])
            o = o * lax.rsqrt(jnp.mean(o * o, axis=-1, keepdims=True) + EPS)
            o = o * go_ref[:, vc] * _silu(r_ref[rows, vc])
            o_ref[rows, vc] = o.astype(BF16)


def _gla_core(qin, kin, kdec, v, r, dec, g_o):
    b, s, dk = qin.shape
    dv = v.shape[2]
    tm = TOKEN_TILE
    tok = lambda width: pl.BlockSpec((None, tm, width), lambda bi, si: (bi, si, 0))
    return pl.pallas_call(
        _gla_core_kernel,
        grid=(b, s // tm),
        in_specs=[tok(dk), tok(dk), tok(dk), tok(dv), tok(dv),
                  pl.BlockSpec((None, tm // CHUNK, dk), lambda bi, si: (bi, si, 0)),
                  _resident(g_o.shape)],
        out_specs=tok(dv),
        out_shape=jax.ShapeDtypeStruct((b, s, dv), BF16),
        scratch_shapes=[pltpu.VMEM((GLA_HEADS, dv // GLA_HEADS, dk // GLA_HEADS), F32)],
        compiler_params=_params(2),
        name="gla_core",
    )(qin, kin, kdec, v, r, dec, g_o)


def _mix_ffn_kernel(x_ref, o_ref, mod_ref, g_ref, wo_ref, win_ref, wout_ref, fg_ref,
                    out_ref, acc_ref, *, hidden, final):
    out_ref[...] = x_ref[...] + mod_ref[2:3, :] * _dot(o_ref[...], wo_ref[...])
    h = _rms_mod(out_ref[...], g_ref[...], mod_ref[3:4, :], mod_ref[4:5, :]).astype(BF16)
    for j in range(hidden // FFN_CHUNK):
        cols = slice(j * FFN_CHUNK, (j + 1) * FFN_CHUNK)
        up_cols = slice(hidden + j * FFN_CHUNK, hidden + (j + 1) * FFN_CHUNK)
        act = (_silu(_dot(h, win_ref[:, cols])) * _dot(h, win_ref[:, up_cols])).astype(BF16)
        part = _dot(act, wout_ref[cols, :])
        if j == 0:
            acc_ref[...] = part
        else:
            acc_ref[...] += part
    y = out_ref[...] + mod_ref[5:6, :] * acc_ref[...]
    if final:
        y = y * lax.rsqrt(jnp.mean(y * y, axis=-1, keepdims=True) + EPS) * fg_ref[...]
    out_ref[...] = y


def _mix_ffn(x, o, mod, g, wo, win, wout, fg, layer, final):
    b, s, d = x.shape
    tm = TOKEN_TILE
    tok = pl.BlockSpec((None, tm, d), lambda bi, si: (bi, si, 0))
    return pl.pallas_call(
        functools.partial(_mix_ffn_kernel, hidden=wout.shape[0], final=final),
        grid=(b, s // tm),
        in_specs=[
            tok, tok,
            pl.BlockSpec((None, None, N_MOD, d), lambda bi, si: (layer, bi, 0, 0)),
            pl.BlockSpec((None, 1, d), lambda bi, si: (layer, 0, 0)),
            _resident(wo.shape), _resident(win.shape), _resident(wout.shape),
            _resident(fg.shape),
        ],
        out_specs=tok,
        out_shape=jax.ShapeDtypeStruct((b, s, d), F32),
        scratch_shapes=[pltpu.VMEM((tm, d), F32)],
        compiler_params=_params(2),
        name="mix_ffn",
    )(x, o, mod, g, wo, win, wout, fg)


def kernel(x, c, ada_w, ada_b, norm1_g, norm2_g, ffn_w_in, ffn_w_out, fox_w_in, fox_b_f, fox_w_out,
           gla_w_in, gla_w_a2, gla_b_a, gla_g_o, gla_w_out, final_g):
    depth, d, _ = ada_w.shape
    b = x.shape[0]
    assert x.shape[1] % TOKEN_TILE == 0 and TOKEN_TILE % ATTN_TILE == 0
    assert ffn_w_out.shape[1] % FFN_CHUNK == 0
    mod = _adaln_mod(c, ada_w, ada_b).reshape(depth, b, N_MOD, d)
    g1 = norm1_g.reshape(depth, 1, d)
    g2 = norm2_g.reshape(depth, 1, d)
    fg = final_g.reshape(1, d)
    for i in range(depth):
        j = i // 2
        if i % 2 == 0:
            w = fox_w_in[j]
            w_f = w[:, 3 * d:]
            pad = jnp.zeros((d, LANES - N_SPLIT * FOX_HEADS), w.dtype)
            w = jnp.concatenate([w[:, :3 * d]] + [w_f] * N_SPLIT + [pad], axis=1).astype(BF16)
            b_f = jnp.concatenate(
                [fox_b_f[j]] * N_SPLIT + [jnp.zeros((LANES - N_SPLIT * FOX_HEADS,), F32)])
            q, k, v, qx, kx = _fox_proj(x, mod, g1, w, b_f.reshape(1, LANES), i)
            o = _fox_attn(q, qx, k, kx, v)
            w_o = fox_w_out[j]
        else:
            w = gla_w_in[j]
            rank = gla_w_a2.shape[1]
            n_main = w.shape[1] - rank
            wa = jnp.pad(w[:, n_main:], ((0, 0), (0, LANES - rank))).astype(BF16)
            wa2 = jnp.pad(gla_w_a2[j], ((0, LANES - rank), (0, 0))).astype(BF16)
            qin, kin, kdec, v, r, dec = _gla_proj(
                x, mod, g1, w[:, :n_main].astype(BF16), wa, wa2, gla_b_a[j].reshape(1, -1), i)
            o = _gla_core(qin, kin, kdec, v, r, dec, gla_g_o[j].reshape(1, -1))
            w_o = gla_w_out[j]
        x = _mix_ffn(x, o, mod, g2, w_o.astype(BF16), ffn_w_in[i].astype(BF16),
                     ffn_w_out[i].astype(BF16), fg, i, final=(i == depth - 1))
    return x
```

```python
import functools

import jax
import jax.numpy as jnp
from jax import lax
from jax.experimental import pallas as pl
from jax.experimental.pallas import tpu as pltpu

F32 = jnp.float32
BF16 = jnp.bfloat16

EPS = 1e-6
N_MOD = 6
CHUNK = 64
FOX_HEADS = 16
GLA_HEADS = 4
GLA_TAU = 16.0
LANES = 128
BIAS_LANES = 8
N_SPLIT = 3
SUM_ROWS = 16

TOKEN_TILE = 512
ATTN_TILE = 256
FFN_CHUNK = 256
CUM_BLOCK = 128
VMEM_LIMIT = 56 * 1024 * 1024


def _params(n_grid):
    return pltpu.CompilerParams(
        dimension_semantics=("arbitrary",) * n_grid, vmem_limit_bytes=VMEM_LIMIT)


def _resident(shape):
    return pl.BlockSpec(shape, lambda *_: (0,) * len(shape), pipeline_mode=pl.Buffered(1))


def _rms_mod(x, g, shift, scale):
    y = x * lax.rsqrt(jnp.mean(x * x, axis=-1, keepdims=True) + EPS) * g
    return y * (1.0 + scale) + shift


def _log_sigmoid(x):
    return jnp.minimum(x, 0.0) - jnp.log1p(jnp.exp(-jnp.abs(x)))


def _silu(x):
    return x * jax.nn.sigmoid(x)


def _split_bf16(x):
    terms = []
    for _ in range(N_SPLIT - 1):
        t = x.astype(BF16)
        terms.append(t)
        x = x - t.astype(F32)
    terms.append(x.astype(BF16))
    return terms


def _dot(a, b):
    return jnp.dot(a, b, preferred_element_type=F32)


def _dot_nt(a, b):
    return lax.dot_general(a, b, (((1,), (1,)), ((), ())), preferred_element_type=F32)


def _dot_tn(a, b):
    return lax.dot_general(a, b, (((0,), (0,)), ((), ())), preferred_element_type=F32)


def _adaln_kernel(c_ref, w_ref, b_ref, o_ref):
    ca = _silu(c_ref[...]).astype(BF16)
    o_ref[...] = _dot(ca, w_ref[...].astype(BF16)) + b_ref[...]


def _adaln_mod(c, ada_w, ada_b):
    depth, d, n = ada_w.shape
    b = c.shape[0]
    bn = 1024
    return pl.pallas_call(
        _adaln_kernel,
        grid=(depth, n // bn),
        in_specs=[
            pl.BlockSpec((b, d), lambda i, j: (0, 0)),
            pl.BlockSpec((None, d, bn), lambda i, j: (i, 0, j)),
            pl.BlockSpec((None, 1, bn), lambda i, j: (i, 0, j)),
        ],
        out_specs=pl.BlockSpec((None, b, bn), lambda i, j: (i, 0, j)),
        out_shape=jax.ShapeDtypeStruct((depth, b, n), F32),
        compiler_params=_params(2),
        name="adaln_mod",
    )(c, ada_w, ada_b.reshape(depth, 1, n))


def _fox_proj_kernel(x_ref, mod_ref, g_ref, w_ref, bf_ref,
                     q_ref, k_ref, vt_ref, qx_ref, kx_ref, carry_ref, *, d, scale):
    tm = x_ref.shape[0]

    @pl.when(pl.program_id(1) == 0)
    def _():
        carry_ref[...] = jnp.zeros_like(carry_ref)

    h = _rms_mod(x_ref[...], g_ref[...], mod_ref[0:1, :], mod_ref[1:2, :]).astype(BF16)
    proj = _dot(h, w_ref[...])
    q_ref[...] = (proj[:, :d] * scale).astype(BF16)
    k_ref[...] = proj[:, d:2 * d].astype(BF16)
    vt_ref[...] = proj[:, 2 * d:3 * d].T.astype(BF16)

    lane = lax.broadcasted_iota(jnp.int32, (1, LANES), 1)
    used = lane < N_SPLIT * FOX_HEADS
    log_f = jnp.where(used, _log_sigmoid(proj[:, 3 * d:] + bf_ref[...]), 0.0)
    row = lax.broadcasted_iota(jnp.int32, (tm, tm), 0)
    col = lax.broadcasted_iota(jnp.int32, (tm, tm), 1)
    tril = jnp.where(row >= col, 1.0, 0.0).astype(BF16)
    sums = _dot(tril, jnp.concatenate(_split_bf16(log_f), axis=1))
    cum = carry_ref[...] + sum(sums[:, t * LANES:(t + 1) * LANES] for t in range(N_SPLIT))
    carry_ref[...] = cum[tm - 1:tm, :]

    terms = _split_bf16(cum)
    group = lane // FOX_HEADS
    packed = jnp.zeros_like(terms[0])
    for t in range(N_SPLIT):
        packed = jnp.where(group == t, terms[t], packed)
    src = lax.broadcasted_iota(jnp.int32, (LANES, LANES), 0)
    dst = lax.broadcasted_iota(jnp.int32, (LANES, LANES), 1)
    same_head = (src % FOX_HEADS == dst // BIAS_LANES) & (src < N_SPLIT * FOX_HEADS)
    place_q = jnp.where(same_head & (src // FOX_HEADS == dst % BIAS_LANES), 1.0, 0.0)
    place_k = jnp.where(same_head & (src // FOX_HEADS + N_SPLIT == dst % BIAS_LANES), 1.0, 0.0)
    pos = lane % BIAS_LANES
    ones_q = jnp.where((pos >= N_SPLIT) & (pos < 2 * N_SPLIT), 1.0, 0.0)
    ones_k = jnp.where(pos < N_SPLIT, 1.0, 0.0)
    qx_ref[...] = (_dot(packed, place_q.astype(BF16)) + ones_q).astype(BF16)
    kx_ref[...] = (ones_k - _dot(packed, place_k.astype(BF16))).astype(BF16)


def _fox_proj(x, mod, g, w, b_f, layer):
    b, s, d = x.shape
    tm = TOKEN_TILE
    tok = lambda width: pl.BlockSpec((None, tm, width), lambda bi, si: (bi, si, 0))
    out_bf16 = lambda width: jax.ShapeDtypeStruct((b, s, width), BF16)
    return pl.pallas_call(
        functools.partial(_fox_proj_kernel, d=d, scale=(d // FOX_HEADS) ** -0.5),
        grid=(b, s // tm),
        in_specs=[
            tok(d),
            pl.BlockSpec((None, None, N_MOD, d), lambda bi, si: (layer, bi, 0, 0)),
            pl.BlockSpec((None, 1, d), lambda bi, si: (layer, 0, 0)),
            _resident(w.shape),
            _resident(b_f.shape),
        ],
        out_specs=[tok(d), tok(d),
                   pl.BlockSpec((None, d, tm), lambda bi, si: (bi, 0, si)),
                   tok(LANES), tok(LANES)],
        out_shape=[out_bf16(d), out_bf16(d),
                   jax.ShapeDtypeStruct((b, d, s), BF16),
                   out_bf16(LANES), out_bf16(LANES)],
        scratch_shapes=[pltpu.VMEM((1, LANES), F32)],
        compiler_params=_params(2),
        name="fox_proj",
    )(x, mod, g, w, b_f)


def _fox_attn_kernel(q_ref, qx_ref, k_ref, kx_ref, vt_ref, o_ref, s_ref):
    tq = q_ref.shape[0]
    hd = LANES // 2
    pair = pl.program_id(1)
    qi = pl.program_id(2)
    lane = lax.broadcasted_iota(jnp.int32, (1, 2 * LANES), 1)
    q_full = jnp.concatenate([q_ref[...], qx_ref[...]], axis=1)
    key = lax.broadcasted_iota(jnp.int32, (tq, tq), 0)
    qry = lax.broadcasted_iota(jnp.int32, (tq, tq), 1)
    causal = key <= qry

    qh = []
    for hh in range(2):
        bias_lo = LANES + BIAS_LANES * (2 * pair + hh)
        keep = (((lane >= hh * hd) & (lane < (hh + 1) * hd))
                | ((lane >= bias_lo) & (lane < bias_lo + BIAS_LANES)))
        qh.append(jnp.where(keep, q_full, jnp.zeros_like(q_full)))

    ones = jnp.ones((SUM_ROWS, tq), BF16)

    def attend(n):
        items = [(hh, j) for j in [n] + list(range(n)) for hh in range(2)]

        def scores(t):
            hh, j = items[t]
            rows = slice(j * tq, (j + 1) * tq)
            kb = jnp.concatenate([k_ref[rows, :], kx_ref[rows, :]], axis=1)
            s_ref[t] = _dot_nt(kb, qh[hh])

        scores(0)
        m = [None, None]
        acc = [None, None]
        for t, (hh, j) in enumerate(items):
            if t + 1 < len(items):
                scores(t + 1)
            rows = slice(j * tq, (j + 1) * tq)
            s = s_ref[t]
            if j == n:
                s = jnp.where(causal, s, -jnp.inf)
            m_new = jnp.max(s, axis=0, keepdims=True)
            if m[hh] is not None:
                m_new = jnp.maximum(m[hh], m_new)
            vt = jnp.concatenate([vt_ref[hh * hd:(hh + 1) * hd, rows], ones], axis=0)
            pv = _dot(vt, jnp.exp(s - m_new).astype(BF16))
            acc[hh] = pv if m[hh] is None else jnp.exp(m[hh] - m_new) * acc[hh] + pv
            m[hh] = m_new
        o_t = [a[:hd, :] / a[hd:hd + 1, :] for a in acc]
        o_ref[...] = jnp.concatenate(o_t, axis=0).T.astype(BF16)

    for n in range(k_ref.shape[0] // tq):
        pl.when(qi == n)(functools.partial(attend, n))


def _fox_attn(q, qx, k, kx, vt):
    b, s, d = q.shape
    tq = ATTN_TILE
    q_blk = lambda col: pl.BlockSpec((None, tq, LANES), col)
    kv_blk = lambda col: pl.BlockSpec((None, s, LANES), col)
    return pl.pallas_call(
        _fox_attn_kernel,
        grid=(b, d // LANES, s // tq),
        in_specs=[
            q_blk(lambda bi, p, qi: (bi, qi, p)),
            q_blk(lambda bi, p, qi: (bi, qi, 0)),
            kv_blk(lambda bi, p, qi: (bi, 0, p)),
            kv_blk(lambda bi, p, qi: (bi, 0, 0)),
            pl.BlockSpec((None, LANES, s), lambda bi, p, qi: (bi, p, 0)),
        ],
        out_specs=q_blk(lambda bi, p, qi: (bi, qi, p)),
        out_shape=jax.ShapeDtypeStruct((b, s, d), BF16),
        scratch_shapes=[pltpu.VMEM((2 * (s // tq), tq, tq), F32)],
        compiler_params=_params(3),
        name="fox_attn",
    )(q, qx, k, kx, vt)


def _gla_proj_kernel(x_ref, mod_ref, g_ref, w_ref, wa_ref, wa2_ref, ba_ref,
                     qin_ref, kin_ref, kdec_ref, v_ref, r_ref, dec_ref, b_ref, *, dk, dv, scale):
    tm = x_ref.shape[0]
    h = _rms_mod(x_ref[...], g_ref[...], mod_ref[0:1, :], mod_ref[1:2, :]).astype(BF16)
    proj = _dot(h, w_ref[...])
    v_ref[...] = proj[:, 2 * dk:2 * dk + dv].astype(BF16)
    r_ref[...] = proj[:, 2 * dk + dv:]
    a_lr = _dot(h, wa_ref[...]).astype(BF16)
    log_alpha = _log_sigmoid(_dot(a_lr, wa2_ref[...]) + ba_ref[...]) / GLA_TAU

    row = lax.broadcasted_iota(jnp.int32, (CUM_BLOCK, CUM_BLOCK), 0)
    col = lax.broadcasted_iota(jnp.int32, (CUM_BLOCK, CUM_BLOCK), 1)
    tril = jnp.where((row >= col) & (row // CHUNK == col // CHUNK), 1.0, 0.0).astype(BF16)
    terms = jnp.concatenate(_split_bf16(log_alpha), axis=1)
    for i in range(tm // CUM_BLOCK):
        rows = slice(i * CUM_BLOCK, (i + 1) * CUM_BLOCK)
        sums = _dot(tril, terms[rows, :])
        b_ref[rows, :] = sum(sums[:, t * dk:(t + 1) * dk] for t in range(N_SPLIT))

    for c in range(tm // CHUNK):
        rows = slice(c * CHUNK, (c + 1) * CHUNK)
        b = b_ref[rows, :]
        b_last = b_ref[(c + 1) * CHUNK - 1:(c + 1) * CHUNK, :]
        k = proj[rows, dk:2 * dk]
        qin_ref[rows, :] = (proj[rows, :dk] * scale * jnp.exp(b)).astype(BF16)
        kin_ref[rows, :] = (k * jnp.exp(-b)).astype(BF16)
        kdec_ref[rows, :] = (k * jnp.exp(b_last - b)).astype(BF16)
        dec_ref[c:c + 1, :] = jnp.exp(b_last)


def _gla_proj(x, mod, g, w, wa, wa2, ba, layer):
    b, s, d = x.shape
    tm = TOKEN_TILE
    dk = wa2.shape[1]
    dv = (w.shape[1] - 2 * dk) // 2
    tok = lambda width: pl.BlockSpec((None, tm, width), lambda bi, si: (bi, si, 0))
    out = lambda width, dt: jax.ShapeDtypeStruct((b, s, width), dt)
    return pl.pallas_call(
        functools.partial(_gla_proj_kernel, dk=dk, dv=dv, scale=(dk // GLA_HEADS) ** -0.5),
        grid=(b, s // tm),
        in_specs=[
            tok(d),
            pl.BlockSpec((None, None, N_MOD, d), lambda bi, si: (layer, bi, 0, 0)),
            pl.BlockSpec((None, 1, d), lambda bi, si: (layer, 0, 0)),
            _resident(w.shape),
            _resident(wa.shape),
            _resident(wa2.shape),
            _resident(ba.shape),
        ],
        out_specs=[tok(dk), tok(dk), tok(dk), tok(dv), tok(dv),
                   pl.BlockSpec((None, tm // CHUNK, dk), lambda bi, si: (bi, si, 0))],
        out_shape=[out(dk, BF16), out(dk, BF16), out(dk, BF16), out(dv, BF16), out(dv, F32),
                   jax.ShapeDtypeStruct((b, s // CHUNK, dk), F32)],
        scratch_shapes=[pltpu.VMEM((tm, dk), F32)],
        compiler_params=_params(2),
        name="gla_proj",
    )(x, mod, g, w, wa, wa2, ba)


def _gla_core_kernel(qin_ref, kin_ref, kdec_ref, v_ref, r_ref, dec_ref, go_ref,
                     o_ref, state_ref):
    tm = qin_ref.shape[0]
    hk = qin_ref.shape[1] // GLA_HEADS
    hv = v_ref.shape[1] // GLA_HEADS

    @pl.when(pl.program_id(1) == 0)
    def _():
        state_ref[...] = jnp.zeros_like(state_ref)

    row = lax.broadcasted_iota(jnp.int32, (CHUNK, CHUNK), 0)
    col = lax.broadcasted_iota(jnp.int32, (CHUNK, CHUNK), 1)
    causal = row >= col

    for c in range(tm // CHUNK):
        rows = slice(c * CHUNK, (c + 1) * CHUNK)
        for hd in range(GLA_HEADS):
            kc = slice(hd * hk, (hd + 1) * hk)
            vc = slice(hd * hv, (hd + 1) * hv)
            q = qin_ref[rows, kc]
            v = v_ref[rows, vc]
            a = jnp.where(causal, _dot_nt(q, kin_ref[rows, kc]), 0.0).astype(BF16)
            state_t = state_ref[hd]
            o = _dot(a, v) + _dot_nt(q, state_t.astype(BF16))
            state_ref[hd] = state_t * dec_ref[c:c + 1, kc] + _dot_tn(v, kdec_ref[rows, kc])
            o = o * lax.rsqrt(jnp.mean(o * o, axis=-1, keepdims=True) + EPS)
            o = o * go_ref[:, vc] * _silu(r_ref[rows, vc])
            o_ref[rows, vc] = o.astype(BF16)


def _gla_core(qin, kin, kdec, v, r, dec, g_o):
    b, s, dk = qin.shape
    dv = v.shape[2]
    tm = TOKEN_TILE
    tok = lambda width: pl.BlockSpec((None, tm, width), lambda bi, si: (bi, si, 0))
    return pl.pallas_call(
        _gla_core_kernel,
        grid=(b, s // tm),
        in_specs=[tok(dk), tok(dk), tok(dk), tok(dv), tok(dv),
                  pl.BlockSpec((None, tm // CHUNK, dk), lambda bi, si: (bi, si, 0)),
                  _resident(g_o.shape)],
        out_specs=tok(dv),
        out_shape=jax.ShapeDtypeStruct((b, s, dv), BF16),
        scratch_shapes=[pltpu.VMEM((GLA_HEADS, dv // GLA_HEADS, dk // GLA_HEADS), F32)],
        compiler_params=_params(2),
        name="gla_core",
    )(qin, kin, kdec, v, r, dec, g_o)


def _mix_ffn_kernel(x_ref, o_ref, mod_ref, g_ref, wo_ref, win_ref, wout_ref, fg_ref,
                    out_ref, acc_ref, *, hidden, final):
    out_ref[...] = x_ref[...] + mod_ref[2:3, :] * _dot(o_ref[...], wo_ref[...])
    h = _rms_mod(out_ref[...], g_ref[...], mod_ref[3:4, :], mod_ref[4:5, :]).astype(BF16)
    for j in range(hidden // FFN_CHUNK):
        cols = slice(j * FFN_CHUNK, (j + 1) * FFN_CHUNK)
        up_cols = slice(hidden + j * FFN_CHUNK, hidden + (j + 1) * FFN_CHUNK)
        act = (_silu(_dot(h, win_ref[:, cols])) * _dot(h, win_ref[:, up_cols])).astype(BF16)
        part = _dot(act, wout_ref[cols, :])
        if j == 0:
            acc_ref[...] = part
        else:
            acc_ref[...] += part
    y = out_ref[...] + mod_ref[5:6, :] * acc_ref[...]
    if final:
        y = y * lax.rsqrt(jnp.mean(y * y, axis=-1, keepdims=True) + EPS) * fg_ref[...]
    out_ref[...] = y


def _mix_ffn(x, o, mod, g, wo, win, wout, fg, layer, final):
    b, s, d = x.shape
    tm = TOKEN_TILE
    tok = pl.BlockSpec((None, tm, d), lambda bi, si: (bi, si, 0))
    return pl.pallas_call(
        functools.partial(_mix_ffn_kernel, hidden=wout.shape[0], final=final),
        grid=(b, s // tm),
        in_specs=[
            tok, tok,
            pl.BlockSpec((None, None, N_MOD, d), lambda bi, si: (layer, bi, 0, 0)),
            pl.BlockSpec((None, 1, d), lambda bi, si: (layer, 0, 0)),
            _resident(wo.shape), _resident(win.shape), _resident(wout.shape),
            _resident(fg.shape),
        ],
        out_specs=tok,
        out_shape=jax.ShapeDtypeStruct((b, s, d), F32),
        scratch_shapes=[pltpu.VMEM((tm, d), F32)],
        compiler_params=_params(2),
        name="mix_ffn",
    )(x, o, mod, g, wo, win, wout, fg)


def kernel(x, c, ada_w, ada_b, norm1_g, norm2_g, ffn_w_in, ffn_w_out, fox_w_in, fox_b_f, fox_w_out,
           gla_w_in, gla_w_a2, gla_b_a, gla_g_o, gla_w_out, final_g):
    depth, d, _ = ada_w.shape
    b = x.shape[0]
    assert x.shape[1] % TOKEN_TILE == 0 and TOKEN_TILE % ATTN_TILE == 0
    assert ffn_w_out.shape[1] % FFN_CHUNK == 0
    mod = _adaln_mod(c, ada_w, ada_b).reshape(depth, b, N_MOD, d)
    g1 = norm1_g.reshape(depth, 1, d)
    g2 = norm2_g.reshape(depth, 1, d)
    fg = final_g.reshape(1, d)
    for i in range(depth):
        j = i // 2
        if i % 2 == 0:
            w = fox_w_in[j]
            w_f = w[:, 3 * d:]
            pad = jnp.zeros((d, LANES - N_SPLIT * FOX_HEADS), w.dtype)
            w = jnp.concatenate([w[:, :3 * d]] + [w_f] * N_SPLIT + [pad], axis=1).astype(BF16)
            b_f = jnp.concatenate(
                [fox_b_f[j]] * N_SPLIT + [jnp.zeros((LANES - N_SPLIT * FOX_HEADS,), F32)])
            q, k, vt, qx, kx = _fox_proj(x, mod, g1, w, b_f.reshape(1, LANES), i)
            o = _fox_attn(q, qx, k, kx, vt)
            w_o = fox_w_out[j]
        else:
            w = gla_w_in[j]
            rank = gla_w_a2.shape[1]
            n_main = w.shape[1] - rank
            wa = jnp.pad(w[:, n_main:], ((0, 0), (0, LANES - rank))).astype(BF16)
            wa2 = jnp.pad(gla_w_a2[j], ((0, LANES - rank), (0, 0))).astype(BF16)
            qin, kin, kdec, v, r, dec = _gla_proj(
                x, mod, g1, w[:, :n_main].astype(BF16), wa, wa2, gla_b_a[j].reshape(1, -1), i)
            o = _gla_core(qin, kin, kdec, v, r, dec, gla_g_o[j].reshape(1, -1))
            w_o = gla_w_out[j]
        x = _mix_ffn(x, o, mod, g2, w_o.astype(BF16), ffn_w_in[i].astype(BF16),
                     ffn_w_out[i].astype(BF16), fg, i, final=(i == depth - 1))
    return x
```

```python
import functools

import jax
import jax.numpy as jnp
from jax import lax
from jax.experimental import pallas as pl
from jax.experimental.pallas import tpu as pltpu

F32 = jnp.float32
BF16 = jnp.bfloat16

EPS = 1e-6
N_MOD = 6
CHUNK = 64
FOX_HEADS = 16
GLA_HEADS = 4
GLA_TAU = 16.0
LANES = 128
BIAS_LANES = 8
N_SPLIT = 3
SUM_ROWS = 16

TOKEN_TILE = 512
ATTN_TILE = 256
PIPE_DEPTH = 8
FFN_CHUNK = 256
CUM_BLOCK = 128
VMEM_LIMIT = 56 * 1024 * 1024


def _params(n_grid, flags=None):
    return pltpu.CompilerParams(
        dimension_semantics=("arbitrary",) * n_grid, vmem_limit_bytes=VMEM_LIMIT, flags=flags)


def _resident(shape):
    return pl.BlockSpec(shape, lambda *_: (0,) * len(shape), pipeline_mode=pl.Buffered(1))


def _rms_mod(x, g, shift, scale):
    y = x * lax.rsqrt(jnp.mean(x * x, axis=-1, keepdims=True) + EPS) * g
    return y * (1.0 + scale) + shift


def _log_sigmoid(x):
    return jnp.minimum(x, 0.0) - jnp.log1p(jnp.exp(-jnp.abs(x)))


def _silu(x):
    return x * jax.nn.sigmoid(x)


def _split_bf16(x):
    terms = []
    for _ in range(N_SPLIT - 1):
        t = x.astype(BF16)
        terms.append(t)
        x = x - t.astype(F32)
    terms.append(x.astype(BF16))
    return terms


def _dot(a, b):
    return jnp.dot(a, b, preferred_element_type=F32)


def _dot_nt(a, b):
    return lax.dot_general(a, b, (((1,), (1,)), ((), ())), preferred_element_type=F32)


def _dot_tn(a, b):
    return lax.dot_general(a, b, (((0,), (0,)), ((), ())), preferred_element_type=F32)


def _adaln_kernel(c_ref, w_ref, b_ref, o_ref):
    ca = _silu(c_ref[...]).astype(BF16)
    o_ref[...] = _dot(ca, w_ref[...].astype(BF16)) + b_ref[...]


def _adaln_mod(c, ada_w, ada_b):
    depth, d, n = ada_w.shape
    b = c.shape[0]
    bn = 1024
    return pl.pallas_call(
        _adaln_kernel,
        grid=(depth, n // bn),
        in_specs=[
            pl.BlockSpec((b, d), lambda i, j: (0, 0)),
            pl.BlockSpec((None, d, bn), lambda i, j: (i, 0, j)),
            pl.BlockSpec((None, 1, bn), lambda i, j: (i, 0, j)),
        ],
        out_specs=pl.BlockSpec((None, b, bn), lambda i, j: (i, 0, j)),
        out_shape=jax.ShapeDtypeStruct((depth, b, n), F32),
        compiler_params=_params(2),
        name="adaln_mod",
    )(c, ada_w, ada_b.reshape(depth, 1, n))


def _fox_proj_kernel(x_ref, mod_ref, g_ref, w_ref, bf_ref,
                     q_ref, k_ref, vt_ref, qx_ref, kx_ref, carry_ref, *, d, scale):
    tm = x_ref.shape[0]

    @pl.when(pl.program_id(1) == 0)
    def _():
        carry_ref[...] = jnp.zeros_like(carry_ref)

    h = _rms_mod(x_ref[...], g_ref[...], mod_ref[0:1, :], mod_ref[1:2, :]).astype(BF16)
    proj = _dot(h, w_ref[...])
    q_ref[...] = (proj[:, :d] * scale).astype(BF16)
    k_ref[...] = proj[:, d:2 * d].astype(BF16)
    vt_ref[...] = proj[:, 2 * d:3 * d].T.astype(BF16)

    lane = lax.broadcasted_iota(jnp.int32, (1, LANES), 1)
    used = lane < N_SPLIT * FOX_HEADS
    log_f = jnp.where(used, _log_sigmoid(proj[:, 3 * d:] + bf_ref[...]), 0.0)
    row = lax.broadcasted_iota(jnp.int32, (tm, tm), 0)
    col = lax.broadcasted_iota(jnp.int32, (tm, tm), 1)
    tril = jnp.where(row >= col, 1.0, 0.0).astype(BF16)
    sums = _dot(tril, jnp.concatenate(_split_bf16(log_f), axis=1))
    cum = carry_ref[...] + sum(sums[:, t * LANES:(t + 1) * LANES] for t in range(N_SPLIT))
    carry_ref[...] = cum[tm - 1:tm, :]

    terms = _split_bf16(cum)
    group = lane // FOX_HEADS
    packed = jnp.zeros_like(terms[0])
    for t in range(N_SPLIT):
        packed = jnp.where(group == t, terms[t], packed)
    src = lax.broadcasted_iota(jnp.int32, (LANES, LANES), 0)
    dst = lax.broadcasted_iota(jnp.int32, (LANES, LANES), 1)
    same_head = (src % FOX_HEADS == dst // BIAS_LANES) & (src < N_SPLIT * FOX_HEADS)
    place_q = jnp.where(same_head & (src // FOX_HEADS == dst % BIAS_LANES), 1.0, 0.0)
    place_k = jnp.where(same_head & (src // FOX_HEADS + N_SPLIT == dst % BIAS_LANES), 1.0, 0.0)
    pos = lane % BIAS_LANES
    ones_q = jnp.where((pos >= N_SPLIT) & (pos < 2 * N_SPLIT), 1.0, 0.0)
    ones_k = jnp.where(pos < N_SPLIT, 1.0, 0.0)
    qx_ref[...] = (_dot(packed, place_q.astype(BF16)) + ones_q).astype(BF16)
    kx_ref[...] = (ones_k - _dot(packed, place_k.astype(BF16))).astype(BF16)


def _fox_proj(x, mod, g, w, b_f, layer):
    b, s, d = x.shape
    tm = TOKEN_TILE
    tok = lambda width: pl.BlockSpec((None, tm, width), lambda bi, si: (bi, si, 0))
    out_bf16 = lambda width: jax.ShapeDtypeStruct((b, s, width), BF16)
    return pl.pallas_call(
        functools.partial(_fox_proj_kernel, d=d, scale=(d // FOX_HEADS) ** -0.5),
        grid=(b, s // tm),
        in_specs=[
            tok(d),
            pl.BlockSpec((None, None, N_MOD, d), lambda bi, si: (layer, bi, 0, 0)),
            pl.BlockSpec((None, 1, d), lambda bi, si: (layer, 0, 0)),
            _resident(w.shape),
            _resident(b_f.shape),
        ],
        out_specs=[tok(d), tok(d),
                   pl.BlockSpec((None, d, tm), lambda bi, si: (bi, 0, si)),
                   tok(LANES), tok(LANES)],
        out_shape=[out_bf16(d), out_bf16(d),
                   jax.ShapeDtypeStruct((b, d, s), BF16),
                   out_bf16(LANES), out_bf16(LANES)],
        scratch_shapes=[pltpu.VMEM((1, LANES), F32)],
        compiler_params=_params(2),
        name="fox_proj",
    )(x, mod, g, w, b_f)


def _fox_attn_kernel(q_ref, qx_ref, k_ref, kx_ref, vt_ref, o_ref, s_ref, cmax_ref):
    tq = ATTN_TILE
    hd = LANES // 2
    n_blocks = q_ref.shape[0] // tq
    pair = pl.program_id(1)
    lane = lax.broadcasted_iota(jnp.int32, (1, 2 * LANES), 1)
    key = lax.broadcasted_iota(jnp.int32, (tq, tq), 0)
    qry = lax.broadcasted_iota(jnp.int32, (tq, tq), 1)
    causal = key <= qry
    ones = jnp.ones((SUM_ROWS, tq), BF16)
    keep = []
    for hh in range(2):
        bias_lo = LANES + BIAS_LANES * (2 * pair + hh)
        keep.append(((lane >= hh * hd) & (lane < (hh + 1) * hd))
                    | ((lane >= bias_lo) & (lane < bias_lo + BIAS_LANES)))

    items = [(n, hh, j) for n in range(n_blocks) for j in [n] + list(range(n)) for hh in range(2)]

    def blk(i):
        return slice(i * tq, (i + 1) * tq)

    def scores(t):
        n, hh, j = items[t]
        q_full = jnp.concatenate([q_ref[blk(n), :], qx_ref[blk(n), :]], axis=1)
        qh = jnp.where(keep[hh], q_full, jnp.zeros_like(q_full))
        kb = jnp.concatenate([k_ref[blk(j), :], kx_ref[blk(j), :]], axis=1)
        s = _dot_nt(kb, qh)
        if j == n:
            s = jnp.where(causal, s, -jnp.inf)
        s_ref[t] = s
        cmax_ref[t] = jnp.max(s, axis=0, keepdims=True)

    for t in range(PIPE_DEPTH):
        scores(t)
    state = {}
    for t, (n, hh, j) in enumerate(items):
        if t + PIPE_DEPTH < len(items):
            scores(t + PIPE_DEPTH)
        m_old, acc = state.get((n, hh), (None, None))
        m_new = cmax_ref[t] if m_old is None else jnp.maximum(m_old, cmax_ref[t])
        vt = jnp.concatenate([vt_ref[hh * hd:(hh + 1) * hd, blk(j)], ones], axis=0)
        pv = _dot(vt, jnp.exp(s_ref[t] - m_new).astype(BF16))
        acc = pv if m_old is None else jnp.exp(m_old - m_new) * acc + pv
        state[(n, hh)] = (m_new, acc)
        if j == (n - 1 if n else 0) and hh == 1:
            o_t = [state[(n, h)][1] for h in range(2)]
            o_t = [a[:hd, :] / a[hd:hd + 1, :] for a in o_t]
            o_ref[blk(n), :] = jnp.concatenate(o_t, axis=0).T.astype(BF16)


def _fox_attn(q, qx, k, kx, vt):
    b, s, d = q.shape
    tq = ATTN_TILE
    n_items = (s // tq) * (s // tq + 1)
    seq_blk = lambda col: pl.BlockSpec((None, s, LANES), col)
    return pl.pallas_call(
        _fox_attn_kernel,
        grid=(b, d // LANES),
        in_specs=[
            seq_blk(lambda bi, p: (bi, 0, p)),
            seq_blk(lambda bi, p: (bi, 0, 0)),
            seq_blk(lambda bi, p: (bi, 0, p)),
            seq_blk(lambda bi, p: (bi, 0, 0)),
            pl.BlockSpec((None, LANES, s), lambda bi, p: (bi, p, 0)),
        ],
        out_specs=seq_blk(lambda bi, p: (bi, 0, p)),
        out_shape=jax.ShapeDtypeStruct((b, s, d), BF16),
        scratch_shapes=[pltpu.VMEM((n_items, tq, tq), F32), pltpu.VMEM((n_items, 1, tq), F32)],
        compiler_params=_params(2),
        name="fox_attn",
    )(q, qx, k, kx, vt)


def _gla_proj_kernel(x_ref, mod_ref, g_ref, w_ref, wa_ref, wa2_ref, ba_ref,
                     qin_ref, kin_ref, kdec_ref, v_ref, r_ref, dec_ref, b_ref, *, dk, dv, scale):
    tm = x_ref.shape[0]
    h = _rms_mod(x_ref[...], g_ref[...], mod_ref[0:1, :], mod_ref[1:2, :]).astype(BF16)
    proj = _dot(h, w_ref[...])
    v_ref[...] = proj[:, 2 * dk:2 * dk + dv].astype(BF16)
    r_ref[...] = proj[:, 2 * dk + dv:]
    a_lr = _dot(h, wa_ref[...]).astype(BF16)
    log_alpha = _log_sigmoid(_dot(a_lr, wa2_ref[...]) + ba_ref[...]) / GLA_TAU

    row = lax.broadcasted_iota(jnp.int32, (CUM_BLOCK, CUM_BLOCK), 0)
    col = lax.broadcasted_iota(jnp.int32, (CUM_BLOCK, CUM_BLOCK), 1)
    tril = jnp.where((row >= col) & (row // CHUNK == col // CHUNK), 1.0, 0.0).astype(BF16)
    terms = jnp.concatenate(_split_bf16(log_alpha), axis=1)
    for i in range(tm // CUM_BLOCK):
        rows = slice(i * CUM_BLOCK, (i + 1) * CUM_BLOCK)
        sums = _dot(tril, terms[rows, :])
        b_ref[rows, :] = sum(sums[:, t * dk:(t + 1) * dk] for t in range(N_SPLIT))

    for c in range(tm // CHUNK):
        rows = slice(c * CHUNK, (c + 1) * CHUNK)
        b = b_ref[rows, :]
        b_last = b_ref[(c + 1) * CHUNK - 1:(c + 1) * CHUNK, :]
        k = proj[rows, dk:2 * dk]
        qin_ref[rows, :] = (proj[rows, :dk] * scale * jnp.exp(b)).astype(BF16)
        kin_ref[rows, :] = (k * jnp.exp(-b)).astype(BF16)
        kdec_ref[rows, :] = (k * jnp.exp(b_last - b)).astype(BF16)
        dec_ref[c:c + 1, :] = jnp.exp(b_last)


def _gla_proj(x, mod, g, w, wa, wa2, ba, layer):
    b, s, d = x.shape
    tm = TOKEN_TILE
    dk = wa2.shape[1]
    dv = (w.shape[1] - 2 * dk) // 2
    tok = lambda width: pl.BlockSpec((None, tm, width), lambda bi, si: (bi, si, 0))
    out = lambda width, dt: jax.ShapeDtypeStruct((b, s, width), dt)
    return pl.pallas_call(
        functools.partial(_gla_proj_kernel, dk=dk, dv=dv, scale=(dk // GLA_HEADS) ** -0.5),
        grid=(b, s // tm),
        in_specs=[
            tok(d),
            pl.BlockSpec((None, None, N_MOD, d), lambda bi, si: (layer, bi, 0, 0)),
            pl.BlockSpec((None, 1, d), lambda bi, si: (layer, 0, 0)),
            _resident(w.shape),
            _resident(wa.shape),
            _resident(wa2.shape),
            _resident(ba.shape),
        ],
        out_specs=[tok(dk), tok(dk), tok(dk), tok(dv), tok(dv),
                   pl.BlockSpec((None, tm // CHUNK, dk), lambda bi, si: (bi, si, 0))],
        out_shape=[out(dk, BF16), out(dk, BF16), out(dk, BF16), out(dv, BF16), out(dv, F32),
                   jax.ShapeDtypeStruct((b, s // CHUNK, dk), F32)],
        scratch_shapes=[pltpu.VMEM((tm, dk), F32)],
        compiler_params=_params(2),
        name="gla_proj",
    )(x, mod, g, w, wa, wa2, ba)


def _gla_core_kernel(qin_ref, kin_ref, kdec_ref, v_ref, r_ref, dec_ref, go_ref,
                     o_ref, state_ref):
    tm = qin_ref.shape[0]
    hk = qin_ref.shape[1] // GLA_HEADS
    hv = v_ref.shape[1] // GLA_HEADS

    @pl.when(pl.program_id(1) == 0)
    def _():
        state_ref[...] = jnp.zeros_like(state_ref)

    row = lax.broadcasted_iota(jnp.int32, (CHUNK, CHUNK), 0)
    col = lax.broadcasted_iota(jnp.int32, (CHUNK, CHUNK), 1)
    causal = row >= col

    for c in range(tm // CHUNK):
        rows = slice(c * CHUNK, (c + 1) * CHUNK)
        for hd in range(GLA_HEADS):
            kc = slice(hd * hk, (hd + 1) * hk)
            vc = slice(hd * hv, (hd + 1) * hv)
            q = qin_ref[rows, kc]
            v = v_ref[rows, vc]
            a = jnp.where(causal, _dot_nt(q, kin_ref[rows, kc]), 0.0).astype(BF16)
            state_t = state_ref[hd]
            o = _dot(a, v) + _dot_nt(q, state_t.astype(BF16))
            state_ref[hd] = state_t * dec_ref[c:c + 1, kc] + _dot_tn(v, kdec_ref[rows, kc])
            o = o * lax.rsqrt(jnp.mean(o * o, axis=-1, keepdims=True) + EPS)
            o = o * go_ref[:, vc] * _silu(r_ref[rows, vc])
            o_ref[rows, vc] = o.astype(BF16)


def _gla_core(qin, kin, kdec, v, r, dec, g_o):
    b, s, dk = qin.shape
    dv = v.shape[2]
    tm = TOKEN_TILE
    tok = lambda width: pl.BlockSpec((None, tm, width), lambda bi, si: (bi, si, 0))
    return pl.pallas_call(
        _gla_core_kernel,
        grid=(b, s // tm),
        in_specs=[tok(dk), tok(dk), tok(dk), tok(dv), tok(dv),
                  pl.BlockSpec((None, tm // CHUNK, dk), lambda bi, si: (bi, si, 0)),
                  _resident(g_o.shape)],
        out_specs=tok(dv),
        out_shape=jax.ShapeDtypeStruct((b, s, dv), BF16),
        scratch_shapes=[pltpu.VMEM((GLA_HEADS, dv // GLA_HEADS, dk // GLA_HEADS), F32)],
        compiler_params=_params(2),
        name="gla_core",
    )(qin, kin, kdec, v, r, dec, g_o)


def _mix_ffn_kernel(x_ref, o_ref, mod_ref, g_ref, wo_ref, win_ref, wout_ref, fg_ref,
                    out_ref, acc_ref, *, hidden, final):
    out_ref[...] = x_ref[...] + mod_ref[2:3, :] * _dot(o_ref[...], wo_ref[...])
    h = _rms_mod(out_ref[...], g_ref[...], mod_ref[3:4, :], mod_ref[4:5, :]).astype(BF16)
    for j in range(hidden // FFN_CHUNK):
        cols = slice(j * FFN_CHUNK, (j + 1) * FFN_CHUNK)
        up_cols = slice(hidden + j * FFN_CHUNK, hidden + (j + 1) * FFN_CHUNK)
        act = (_silu(_dot(h, win_ref[:, cols])) * _dot(h, win_ref[:, up_cols])).astype(BF16)
        part = _dot(act, wout_ref[cols, :])
        if j == 0:
            acc_ref[...] = part
        else:
            acc_ref[...] += part
    y = out_ref[...] + mod_ref[5:6, :] * acc_ref[...]
    if final:
        y = y * lax.rsqrt(jnp.mean(y * y, axis=-1, keepdims=True) + EPS) * fg_ref[...]
    out_ref[...] = y


def _mix_ffn(x, o, mod, g, wo, win, wout, fg, layer, final):
    b, s, d = x.shape
    tm = TOKEN_TILE
    tok = pl.BlockSpec((None, tm, d), lambda bi, si: (bi, si, 0))
    return pl.pallas_call(
        functools.partial(_mix_ffn_kernel, hidden=wout.shape[0], final=final),
        grid=(b, s // tm),
        in_specs=[
            tok, tok,
            pl.BlockSpec((None, None, N_MOD, d), lambda bi, si: (layer, bi, 0, 0)),
            pl.BlockSpec((None, 1, d), lambda bi, si: (layer, 0, 0)),
            _resident(wo.shape), _resident(win.shape), _resident(wout.shape),
            _resident(fg.shape),
        ],
        out_specs=tok,
        out_shape=jax.ShapeDtypeStruct((b, s, d), F32),
        scratch_shapes=[pltpu.VMEM((tm, d), F32)],
        compiler_params=_params(2),
        name="mix_ffn",
    )(x, o, mod, g, wo, win, wout, fg)


def kernel(x, c, ada_w, ada_b, norm1_g, norm2_g, ffn_w_in, ffn_w_out, fox_w_in, fox_b_f, fox_w_out,
           gla_w_in, gla_w_a2, gla_b_a, gla_g_o, gla_w_out, final_g):
    depth, d, _ = ada_w.shape
    b = x.shape[0]
    assert x.shape[1] % TOKEN_TILE == 0 and TOKEN_TILE % ATTN_TILE == 0
    assert ffn_w_out.shape[1] % FFN_CHUNK == 0
    mod = _adaln_mod(c, ada_w, ada_b).reshape(depth, b, N_MOD, d)
    g1 = norm1_g.reshape(depth, 1, d)
    g2 = norm2_g.reshape(depth, 1, d)
    fg = final_g.reshape(1, d)
    for i in range(depth):
        j = i // 2
        if i % 2 == 0:
            w = fox_w_in[j]
            w_f = w[:, 3 * d:]
            pad = jnp.zeros((d, LANES - N_SPLIT * FOX_HEADS), w.dtype)
            w = jnp.concatenate([w[:, :3 * d]] + [w_f] * N_SPLIT + [pad], axis=1).astype(BF16)
            b_f = jnp.concatenate(
                [fox_b_f[j]] * N_SPLIT + [jnp.zeros((LANES - N_SPLIT * FOX_HEADS,), F32)])
            q, k, vt, qx, kx = _fox_proj(x, mod, g1, w, b_f.reshape(1, LANES), i)
            o = _fox_attn(q, qx, k, kx, vt)
            w_o = fox_w_out[j]
        else:
            w = gla_w_in[j]
            rank = gla_w_a2.shape[1]
            n_main = w.shape[1] - rank
            wa = jnp.pad(w[:, n_main:], ((0, 0), (0, LANES - rank))).astype(BF16)
            wa2 = jnp.pad(gla_w_a2[j], ((0, LANES - rank), (0, 0))).astype(BF16)
            qin, kin, kdec, v, r, dec = _gla_proj(
                x, mod, g1, w[:, :n_main].astype(BF16), wa, wa2, gla_b_a[j].reshape(1, -1), i)
            o = _gla_core(qin, kin, kdec, v, r, dec, gla_g_o[j].reshape(1, -1))
            w_o = gla_w_out[j]
        x = _mix_ffn(x, o, mod, g2, w_o.astype(BF16), ffn_w_in[i].astype(BF16),
                     ffn_w_out[i].astype(BF16), fg, i, final=(i == depth - 1))
    return x
```

```python
import functools
import math

import jax
import jax.numpy as jnp
from jax import lax
from jax.experimental import pallas as pl
from jax.experimental.pallas import tpu as pltpu

F32 = jnp.float32
BF16 = jnp.bfloat16

EPS = 1e-6
LOG2_E = math.log2(math.e)
N_MOD = 6
CHUNK = 64
FOX_HEADS = 16
GLA_HEADS = 4
GLA_TAU = 16.0
LANES = 128
BIAS_LANES = 8
N_SPLIT = 3
GLA_SPLIT = 2
SUM_ROWS = 16

TOKEN_TILE = 512
ATTN_TILE = 256
PIPE_DEPTH = 8
FFN_CHUNK = 256
CUM_BLOCK = 128
VMEM_LIMIT = 56 * 1024 * 1024


def _params(n_grid, flags=None):
    return pltpu.CompilerParams(
        dimension_semantics=("arbitrary",) * n_grid, vmem_limit_bytes=VMEM_LIMIT, flags=flags)


def _resident(shape):
    return pl.BlockSpec(shape, lambda *_: (0,) * len(shape), pipeline_mode=pl.Buffered(1))


def _rms_mod(x, g, shift, scale):
    y = x * lax.rsqrt(jnp.mean(x * x, axis=-1, keepdims=True) + EPS) * g
    return y * (1.0 + scale) + shift


def _log_sigmoid(x):
    return jnp.minimum(x, 0.0) - jnp.log(1.0 + jnp.exp(-jnp.abs(x)))


def _silu(x):
    return x * jax.nn.sigmoid(x)


def _split_bf16(x, n_terms=N_SPLIT):
    terms = []
    for _ in range(n_terms - 1):
        t = x.astype(BF16)
        terms.append(t)
        x = x - t.astype(F32)
    terms.append(x.astype(BF16))
    return terms


def _zero_after(x):
    bits = pltpu.bitcast(x, jnp.uint32)
    return pltpu.bitcast((bits >> 16) >> 16, F32)


def _dot(a, b):
    return jnp.dot(a, b, preferred_element_type=F32)


def _dot_nt(a, b):
    return lax.dot_general(a, b, (((1,), (1,)), ((), ())), preferred_element_type=F32)


def _dot_tn(a, b):
    return lax.dot_general(a, b, (((0,), (0,)), ((), ())), preferred_element_type=F32)


def _adaln_kernel(c_ref, w_ref, b_ref, o_ref):
    ca = _silu(c_ref[...]).astype(BF16)
    o_ref[...] = _dot(ca, w_ref[...].astype(BF16)) + b_ref[...]


def _adaln_mod(c, ada_w, ada_b):
    depth, d, n = ada_w.shape
    b = c.shape[0]
    bn = 1024
    return pl.pallas_call(
        _adaln_kernel,
        grid=(depth, n // bn),
        in_specs=[
            pl.BlockSpec((b, d), lambda i, j: (0, 0)),
            pl.BlockSpec((None, d, bn), lambda i, j: (i, 0, j)),
            pl.BlockSpec((None, 1, bn), lambda i, j: (i, 0, j)),
        ],
        out_specs=pl.BlockSpec((None, b, bn), lambda i, j: (i, 0, j)),
        out_shape=jax.ShapeDtypeStruct((depth, b, n), F32),
        compiler_params=_params(2),
        name="adaln_mod",
    )(c, ada_w, ada_b.reshape(depth, 1, n))


def _fox_proj_kernel(x_ref, mod_ref, g_ref, w_ref, bf_ref,
                     q_ref, k_ref, vt_ref, qx_ref, kx_ref, carry_ref, *, d, scale):
    tm = x_ref.shape[0]

    @pl.when(pl.program_id(1) == 0)
    def _():
        carry_ref[...] = jnp.zeros_like(carry_ref)

    h = _rms_mod(x_ref[...], g_ref[...], mod_ref[0:1, :], mod_ref[1:2, :]).astype(BF16)
    proj = _dot(h, w_ref[...])
    q_ref[...] = (proj[:, :d] * (scale * LOG2_E)).astype(BF16)
    k_ref[...] = proj[:, d:2 * d].astype(BF16)
    vt_ref[...] = proj[:, 2 * d:3 * d].T.astype(BF16)

    lane = lax.broadcasted_iota(jnp.int32, (1, LANES), 1)
    used = lane < N_SPLIT * FOX_HEADS
    log_f = jnp.where(used, _log_sigmoid(proj[:, 3 * d:] + bf_ref[...]), 0.0)
    row = lax.broadcasted_iota(jnp.int32, (tm, tm), 0)
    col = lax.broadcasted_iota(jnp.int32, (tm, tm), 1)
    tril = jnp.where(row >= col, 1.0, 0.0).astype(BF16)
    sums = _dot(tril, jnp.concatenate(_split_bf16(log_f), axis=1))
    cum = carry_ref[...] + sum(sums[:, t * LANES:(t + 1) * LANES] for t in range(N_SPLIT))
    carry_ref[...] = cum[tm - 1:tm, :]

    terms = _split_bf16(cum * LOG2_E)
    group = lane // FOX_HEADS
    packed = jnp.zeros_like(terms[0])
    for t in range(N_SPLIT):
        packed = jnp.where(group == t, terms[t], packed)
    src = lax.broadcasted_iota(jnp.int32, (LANES, LANES), 0)
    dst = lax.broadcasted_iota(jnp.int32, (LANES, LANES), 1)
    same_head = (src % FOX_HEADS == dst // BIAS_LANES) & (src < N_SPLIT * FOX_HEADS)
    place_q = jnp.where(same_head & (src // FOX_HEADS == dst % BIAS_LANES), 1.0, 0.0)
    place_k = jnp.where(same_head & (src // FOX_HEADS + N_SPLIT == dst % BIAS_LANES), 1.0, 0.0)
    pos = lane % BIAS_LANES
    ones_q = jnp.where((pos >= N_SPLIT) & (pos < 2 * N_SPLIT), 1.0, 0.0)
    ones_k = jnp.where(pos < N_SPLIT, 1.0, 0.0)
    qx_ref[...] = (_dot(packed, place_q.astype(BF16)) + ones_q).astype(BF16)
    kx_ref[...] = (ones_k - _dot(packed, place_k.astype(BF16))).astype(BF16)


def _fox_proj(x, mod, g, w, b_f, layer):
    b, s, d = x.shape
    tm = TOKEN_TILE
    tok = lambda width: pl.BlockSpec((None, tm, width), lambda bi, si: (bi, si, 0))
    out_bf16 = lambda width: jax.ShapeDtypeStruct((b, s, width), BF16)
    return pl.pallas_call(
        functools.partial(_fox_proj_kernel, d=d, scale=(d // FOX_HEADS) ** -0.5),
        grid=(b, s // tm),
        in_specs=[
            tok(d),
            pl.BlockSpec((None, None, N_MOD, d), lambda bi, si: (layer, bi, 0, 0)),
            pl.BlockSpec((None, 1, d), lambda bi, si: (layer, 0, 0)),
            _resident(w.shape),
            _resident(b_f.shape),
        ],
        out_specs=[tok(d), tok(d),
                   pl.BlockSpec((None, d, tm), lambda bi, si: (bi, 0, si)),
                   tok(LANES), tok(LANES)],
        out_shape=[out_bf16(d), out_bf16(d),
                   jax.ShapeDtypeStruct((b, d, s), BF16),
                   out_bf16(LANES), out_bf16(LANES)],
        scratch_shapes=[pltpu.VMEM((1, LANES), F32)],
        compiler_params=_params(2),
        name="fox_proj",
    )(x, mod, g, w, b_f)


def _fox_attn_kernel(q_ref, qx_ref, k_ref, kx_ref, vt_ref, o_ref, s_ref, cmax_ref):
    tq = ATTN_TILE
    hd = LANES // 2
    n_blocks = q_ref.shape[0] // tq
    pair = pl.program_id(1)
    lane = lax.broadcasted_iota(jnp.int32, (1, 2 * LANES), 1)
    key = lax.broadcasted_iota(jnp.int32, (tq, tq), 0)
    qry = lax.broadcasted_iota(jnp.int32, (tq, tq), 1)
    causal = key <= qry
    ones = jnp.ones((SUM_ROWS, tq), BF16)
    keep = []
    for hh in range(2):
        bias_lo = LANES + BIAS_LANES * (2 * pair + hh)
        keep.append(((lane >= hh * hd) & (lane < (hh + 1) * hd))
                    | ((lane >= bias_lo) & (lane < bias_lo + BIAS_LANES)))

    items = [(n, hh, j) for n in range(n_blocks) for j in [n] + list(range(n)) for hh in range(2)]

    def blk(i):
        return slice(i * tq, (i + 1) * tq)

    def scores(t):
        n, hh, j = items[t]
        q_full = jnp.concatenate([q_ref[blk(n), :], qx_ref[blk(n), :]], axis=1)
        qh = jnp.where(keep[hh], q_full, jnp.zeros_like(q_full))
        kb = jnp.concatenate([k_ref[blk(j), :], kx_ref[blk(j), :]], axis=1)
        s = _dot_nt(kb, qh)
        if j == n:
            s = jnp.where(causal, s, -jnp.inf)
        s_ref[t] = s
        cmax_ref[t] = jnp.max(s, axis=0, keepdims=True)

    for t in range(PIPE_DEPTH):
        scores(t)
    state = {}
    for t, (n, hh, j) in enumerate(items):
        if t + PIPE_DEPTH < len(items):
            scores(t + PIPE_DEPTH)
        m_old, acc = state.get((n, hh), (None, None))
        m_new = cmax_ref[t] if m_old is None else jnp.maximum(m_old, cmax_ref[t])
        vt = jnp.concatenate([vt_ref[hh * hd:(hh + 1) * hd, blk(j)], ones], axis=0)
        pv = _dot(vt, jnp.exp2(s_ref[t] - m_new).astype(BF16))
        acc = pv if m_old is None else jnp.exp2(m_old - m_new) * acc + pv
        state[(n, hh)] = (m_new, acc)
        if j == (n - 1 if n else 0) and hh == 1:
            o_t = [state[(n, h)][1] for h in range(2)]
            o_t = [a[:hd, :] / a[hd:hd + 1, :] for a in o_t]
            o_ref[blk(n), :] = jnp.concatenate(o_t, axis=0).T.astype(BF16)


def _fox_attn(q, qx, k, kx, vt):
    b, s, d = q.shape
    tq = ATTN_TILE
    n_items = (s // tq) * (s // tq + 1)
    seq_blk = lambda col: pl.BlockSpec((None, s, LANES), col)
    return pl.pallas_call(
        _fox_attn_kernel,
        grid=(b, d // LANES),
        in_specs=[
            seq_blk(lambda bi, p: (bi, 0, p)),
            seq_blk(lambda bi, p: (bi, 0, 0)),
            seq_blk(lambda bi, p: (bi, 0, p)),
            seq_blk(lambda bi, p: (bi, 0, 0)),
            pl.BlockSpec((None, LANES, s), lambda bi, p: (bi, p, 0)),
        ],
        out_specs=seq_blk(lambda bi, p: (bi, 0, p)),
        out_shape=jax.ShapeDtypeStruct((b, s, d), BF16),
        scratch_shapes=[pltpu.VMEM((n_items, tq, tq), F32), pltpu.VMEM((n_items, 1, tq), F32)],
        compiler_params=_params(2),
        name="fox_attn",
    )(q, qx, k, kx, vt)


def _gla_proj_kernel(x_ref, mod_ref, g_ref, w_ref, wa_ref, wa2_ref, ba_ref,
                     qin_ref, kin_ref, kdec_ref, v_ref, r_ref, dec_ref, b_ref, *, dk, dv, scale):
    tm = x_ref.shape[0]
    h = _rms_mod(x_ref[...], g_ref[...], mod_ref[0:1, :], mod_ref[1:2, :]).astype(BF16)
    a_lr = _dot(h, wa_ref[...]).astype(BF16)
    log_alpha = _log_sigmoid(_dot(a_lr, wa2_ref[...]) + ba_ref[...]) / GLA_TAU
    qk = _dot(h, w_ref[:, :2 * dk])

    row = lax.broadcasted_iota(jnp.int32, (CUM_BLOCK, CUM_BLOCK), 0)
    col = lax.broadcasted_iota(jnp.int32, (CUM_BLOCK, CUM_BLOCK), 1)
    tril = jnp.where((row >= col) & (row // CHUNK == col // CHUNK), 1.0, 0.0).astype(BF16)
    terms = jnp.concatenate(_split_bf16(log_alpha, GLA_SPLIT), axis=1)
    for i in range(tm // CUM_BLOCK):
        rows = slice(i * CUM_BLOCK, (i + 1) * CUM_BLOCK)
        sums = _dot(tril, terms[rows, :])
        b_ref[rows, :] = sum(sums[:, t * dk:(t + 1) * dk] for t in range(GLA_SPLIT))

    n_chunks = tm // CHUNK
    slab = 2 * dv // n_chunks
    for c in range(n_chunks):
        cols = slice(c * slab, (c + 1) * slab)
        vr = _dot(h, w_ref[:, 2 * dk + c * slab:2 * dk + (c + 1) * slab])
        if (c + 1) * slab <= dv:
            v_ref[:, cols] = vr.astype(BF16)
        else:
            r_ref[:, c * slab - dv:(c + 1) * slab - dv] = vr

        zero = jnp.concatenate([_zero_after(vr[tm - 8:, :LANES])[:1, :]] * (dk // LANES), axis=1)

        rows = slice(c * CHUNK, (c + 1) * CHUNK)
        b = b_ref[rows, :] + zero
        b_last = b[CHUNK - 1:, :]
        k = qk[rows, dk:2 * dk]
        qin_ref[rows, :] = (qk[rows, :dk] * scale * jnp.exp(b)).astype(BF16)
        kin_ref[rows, :] = (k * jnp.exp(-b)).astype(BF16)
        kdec_ref[rows, :] = (k * jnp.exp(b_last - b)).astype(BF16)
        dec_ref[c:c + 1, :] = jnp.exp(b_last)


def _gla_proj(x, mod, g, w, wa, wa2, ba, layer):
    b, s, d = x.shape
    tm = TOKEN_TILE
    dk = wa2.shape[1]
    dv = (w.shape[1] - 2 * dk) // 2
    tok = lambda width: pl.BlockSpec((None, tm, width), lambda bi, si: (bi, si, 0))
    out = lambda width, dt: jax.ShapeDtypeStruct((b, s, width), dt)
    return pl.pallas_call(
        functools.partial(_gla_proj_kernel, dk=dk, dv=dv, scale=(dk // GLA_HEADS) ** -0.5),
        grid=(b, s // tm),
        in_specs=[
            tok(d),
            pl.BlockSpec((None, None, N_MOD, d), lambda bi, si: (layer, bi, 0, 0)),
            pl.BlockSpec((None, 1, d), lambda bi, si: (layer, 0, 0)),
            _resident(w.shape),
            _resident(wa.shape),
            _resident(wa2.shape),
            _resident(ba.shape),
        ],
        out_specs=[tok(dk), tok(dk), tok(dk), tok(dv), tok(dv),
                   pl.BlockSpec((None, tm // CHUNK, dk), lambda bi, si: (bi, si, 0))],
        out_shape=[out(dk, BF16), out(dk, BF16), out(dk, BF16), out(dv, BF16), out(dv, F32),
                   jax.ShapeDtypeStruct((b, s // CHUNK, dk), F32)],
        scratch_shapes=[pltpu.VMEM((tm, dk), F32)],
        compiler_params=_params(2),
        name="gla_proj",
    )(x, mod, g, w, wa, wa2, ba)


def _gla_core_kernel(qin_ref, kin_ref, kdec_ref, v_ref, r_ref, dec_ref, go_ref,
                     o_ref, state_ref):
    tm = qin_ref.shape[0]
    hk = qin_ref.shape[1] // GLA_HEADS
    hv = v_ref.shape[1] // GLA_HEADS

    @pl.when(pl.program_id(1) == 0)
    def _():
        state_ref[...] = jnp.zeros_like(state_ref)

    row = lax.broadcasted_iota(jnp.int32, (CHUNK, CHUNK), 0)
    col = lax.broadcasted_iota(jnp.int32, (CHUNK, CHUNK), 1)
    causal = row >= col

    for c in range(tm // CHUNK):
        rows = slice(c * CHUNK, (c + 1) * CHUNK)
        for hd in range(GLA_HEADS):
            kc = slice(hd * hk, (hd + 1) * hk)
            vc = slice(hd * hv, (hd + 1) * hv)
            q = qin_ref[rows, kc]
            v = v_ref[rows, vc]
            a = jnp.where(causal, _dot_nt(q, kin_ref[rows, kc]), 0.0).astype(BF16)
            state_t = state_ref[hd]
            o = _dot(a, v) + _dot_nt(q, state_t.astype(BF16))
            state_ref[hd] = state_t * dec_ref[c:c + 1, kc] + _dot_tn(v, kdec_ref[rows, kc])
            o = o * lax.rsqrt(jnp.mean(o * o, axis=-1, keepdims=True) + EPS)
            o = o * go_ref[:, vc] * _silu(r_ref[rows, vc])
            o_ref[rows, vc] = o.astype(BF16)


def _gla_core(qin, kin, kdec, v, r, dec, g_o):
    b, s, dk = qin.shape
    dv = v.shape[2]
    tm = TOKEN_TILE
    tok = lambda width: pl.BlockSpec((None, tm, width), lambda bi, si: (bi, si, 0))
    return pl.pallas_call(
        _gla_core_kernel,
        grid=(b, s // tm),
        in_specs=[tok(dk), tok(dk), tok(dk), tok(dv), tok(dv),
                  pl.BlockSpec((None, tm // CHUNK, dk), lambda bi, si: (bi, si, 0)),
                  _resident(g_o.shape)],
        out_specs=tok(dv),
        out_shape=jax.ShapeDtypeStruct((b, s, dv), BF16),
        scratch_shapes=[pltpu.VMEM((GLA_HEADS, dv // GLA_HEADS, dk // GLA_HEADS), F32)],
        compiler_params=_params(2),
        name="gla_core",
    )(qin, kin, kdec, v, r, dec, g_o)


def _mix_ffn_kernel(x_ref, o_ref, mod_ref, g_ref, wo_ref, win_ref, wout_ref, fg_ref,
                    out_ref, acc_ref, *, hidden, final):
    out_ref[...] = x_ref[...] + mod_ref[2:3, :] * _dot(o_ref[...], wo_ref[...])
    h = _rms_mod(out_ref[...], g_ref[...], mod_ref[3:4, :], mod_ref[4:5, :]).astype(BF16)
    for j in range(hidden // FFN_CHUNK):
        cols = slice(j * FFN_CHUNK, (j + 1) * FFN_CHUNK)
        up_cols = slice(hidden + j * FFN_CHUNK, hidden + (j + 1) * FFN_CHUNK)
        act = (_silu(_dot(h, win_ref[:, cols])) * _dot(h, win_ref[:, up_cols])).astype(BF16)
        part = _dot(act, wout_ref[cols, :])
        if j == 0:
            acc_ref[...] = part
        else:
            acc_ref[...] += part
    y = out_ref[...] + mod_ref[5:6, :] * acc_ref[...]
    if final:
        y = y * lax.rsqrt(jnp.mean(y * y, axis=-1, keepdims=True) + EPS) * fg_ref[...]
    out_ref[...] = y


def _mix_ffn(x, o, mod, g, wo, win, wout, fg, layer, final):
    b, s, d = x.shape
    tm = TOKEN_TILE
    tok = pl.BlockSpec((None, tm, d), lambda bi, si: (bi, si, 0))
    return pl.pallas_call(
        functools.partial(_mix_ffn_kernel, hidden=wout.shape[0], final=final),
        grid=(b, s // tm),
        in_specs=[
            tok, tok,
            pl.BlockSpec((None, None, N_MOD, d), lambda bi, si: (layer, bi, 0, 0)),
            pl.BlockSpec((None, 1, d), lambda bi, si: (layer, 0, 0)),
            _resident(wo.shape), _resident(win.shape), _resident(wout.shape),
            _resident(fg.shape),
        ],
        out_specs=tok,
        out_shape=jax.ShapeDtypeStruct((b, s, d), F32),
        scratch_shapes=[pltpu.VMEM((tm, d), F32)],
        compiler_params=_params(2),
        name="mix_ffn",
    )(x, o, mod, g, wo, win, wout, fg)


def kernel(x, c, ada_w, ada_b, norm1_g, norm2_g, ffn_w_in, ffn_w_out, fox_w_in, fox_b_f, fox_w_out,
           gla_w_in, gla_w_a2, gla_b_a, gla_g_o, gla_w_out, final_g):
    depth, d, _ = ada_w.shape
    b = x.shape[0]
    assert x.shape[1] % TOKEN_TILE == 0 and TOKEN_TILE % ATTN_TILE == 0
    assert ffn_w_out.shape[1] % FFN_CHUNK == 0
    mod = _adaln_mod(c, ada_w, ada_b).reshape(depth, b, N_MOD, d)
    g1 = norm1_g.reshape(depth, 1, d)
    g2 = norm2_g.reshape(depth, 1, d)
    fg = final_g.reshape(1, d)
    for i in range(depth):
        j = i // 2
        if i % 2 == 0:
            w = fox_w_in[j]
            w_f = w[:, 3 * d:]
            pad = jnp.zeros((d, LANES - N_SPLIT * FOX_HEADS), w.dtype)
            w = jnp.concatenate([w[:, :3 * d]] + [w_f] * N_SPLIT + [pad], axis=1).astype(BF16)
            b_f = jnp.concatenate(
                [fox_b_f[j]] * N_SPLIT + [jnp.zeros((LANES - N_SPLIT * FOX_HEADS,), F32)])
            q, k, vt, qx, kx = _fox_proj(x, mod, g1, w, b_f.reshape(1, LANES), i)
            o = _fox_attn(q, qx, k, kx, vt)
            w_o = fox_w_out[j]
        else:
            w = gla_w_in[j]
            rank = gla_w_a2.shape[1]
            n_main = w.shape[1] - rank
            wa = jnp.pad(w[:, n_main:], ((0, 0), (0, LANES - rank))).astype(BF16)
            wa2 = jnp.pad(gla_w_a2[j], ((0, LANES - rank), (0, 0))).astype(BF16)
            qin, kin, kdec, v, r, dec = _gla_proj(
                x, mod, g1, w[:, :n_main].astype(BF16), wa, wa2, gla_b_a[j].reshape(1, -1), i)
            o = _gla_core(qin, kin, kdec, v, r, dec, gla_g_o[j].reshape(1, -1))
            w_o = gla_w_out[j]
        x = _mix_ffn(x, o, mod, g2, w_o.astype(BF16), ffn_w_in[i].astype(BF16),
                     ffn_w_out[i].astype(BF16), fg, i, final=(i == depth - 1))
    return x
```

```python
import functools
import math

import jax
import jax.numpy as jnp
from jax import lax
from jax.experimental import pallas as pl
from jax.experimental.pallas import tpu as pltpu

F32 = jnp.float32
BF16 = jnp.bfloat16

EPS = 1e-6
LOG2_E = math.log2(math.e)
N_MOD = 6
CHUNK = 64
FOX_HEADS = 16
GLA_HEADS = 4
GLA_TAU = 16.0
LANES = 128
BIAS_LANES = 8
N_SPLIT = 3
GLA_SPLIT = 2
SUM_ROWS = 16

TOKEN_TILE = 512
ATTN_TILE = 256
PIPE_DEPTH = 8
GLA_BATCH = 2
SLABS_AHEAD = 2
FFN_CHUNK = 256
ROW_PARTS = 2
CUM_BLOCK = 128
VMEM_LIMIT = 56 * 1024 * 1024


def _params(n_grid, flags=None):
    return pltpu.CompilerParams(
        dimension_semantics=("arbitrary",) * n_grid, vmem_limit_bytes=VMEM_LIMIT, flags=flags)


def _resident(shape):
    return pl.BlockSpec(shape, lambda *_: (0,) * len(shape), pipeline_mode=pl.Buffered(1))


def _rms_mod(x, g, shift, scale):
    y = x * lax.rsqrt(jnp.mean(x * x, axis=-1, keepdims=True) + EPS) * g
    return y * (1.0 + scale) + shift


def _input_h(x_ref, mod_ref, g_ref, normed):
    if normed:
        return x_ref[...]
    return _rms_mod(x_ref[...], g_ref[...], mod_ref[0:1, :], mod_ref[1:2, :]).astype(BF16)


def _log_sigmoid(x):
    return jnp.minimum(x, 0.0) - jnp.log(1.0 + jnp.exp(-jnp.abs(x)))


def _silu(x):
    return x * jax.nn.sigmoid(x)


def _split_bf16(x, n_terms=N_SPLIT):
    terms = []
    for _ in range(n_terms - 1):
        t = x.astype(BF16)
        terms.append(t)
        x = x - t.astype(F32)
    terms.append(x.astype(BF16))
    return terms


def _dot(a, b):
    return jnp.dot(a, b, preferred_element_type=F32)


def _dot_nt(a, b):
    return lax.dot_general(a, b, (((1,), (1,)), ((), ())), preferred_element_type=F32)


def _dot_tn(a, b):
    return lax.dot_general(a, b, (((0,), (0,)), ((), ())), preferred_element_type=F32)


def _adaln_kernel(c_ref, w_ref, b_ref, o_ref):
    ca = _silu(c_ref[...]).astype(BF16)
    o_ref[...] = _dot(ca, w_ref[...].astype(BF16)) + b_ref[...]


def _adaln_mod(c, ada_w, ada_b):
    depth, d, n = ada_w.shape
    b = c.shape[0]
    bn = 1024
    return pl.pallas_call(
        _adaln_kernel,
        grid=(depth, n // bn),
        in_specs=[
            pl.BlockSpec((b, d), lambda i, j: (0, 0)),
            pl.BlockSpec((None, d, bn), lambda i, j: (i, 0, j)),
            pl.BlockSpec((None, 1, bn), lambda i, j: (i, 0, j)),
        ],
        out_specs=pl.BlockSpec((None, b, bn), lambda i, j: (i, 0, j)),
        out_shape=jax.ShapeDtypeStruct((depth, b, n), F32),
        compiler_params=_params(2),
        name="adaln_mod",
    )(c, ada_w, ada_b.reshape(depth, 1, n))


def _fox_proj_kernel(x_ref, mod_ref, g_ref, w_ref, bf_ref,
                     q_ref, k_ref, vt_ref, qx_ref, kx_ref, carry_ref, *, d, scale, normed):
    tm = x_ref.shape[0]

    @pl.when(pl.program_id(1) == 0)
    def _():
        carry_ref[...] = jnp.zeros_like(carry_ref)

    h = _input_h(x_ref, mod_ref, g_ref, normed)
    proj = _dot(h, w_ref[...])
    q_ref[...] = (proj[:, :d] * (scale * LOG2_E)).astype(BF16)
    k_ref[...] = proj[:, d:2 * d].astype(BF16)
    vt_ref[...] = proj[:, 2 * d:3 * d].T.astype(BF16)

    lane = lax.broadcasted_iota(jnp.int32, (1, LANES), 1)
    used = lane < N_SPLIT * FOX_HEADS
    log_f = jnp.where(used, _log_sigmoid(proj[:, 3 * d:] + bf_ref[...]), 0.0)
    row = lax.broadcasted_iota(jnp.int32, (tm, tm), 0)
    col = lax.broadcasted_iota(jnp.int32, (tm, tm), 1)
    tril = jnp.where(row >= col, 1.0, 0.0).astype(BF16)
    sums = _dot(tril, jnp.concatenate(_split_bf16(log_f), axis=1))
    cum = carry_ref[...] + sum(sums[:, t * LANES:(t + 1) * LANES] for t in range(N_SPLIT))
    carry_ref[...] = cum[tm - 1:tm, :]

    terms = _split_bf16(cum * LOG2_E)
    group = lane // FOX_HEADS
    packed = jnp.zeros_like(terms[0])
    for t in range(N_SPLIT):
        packed = jnp.where(group == t, terms[t], packed)
    src = lax.broadcasted_iota(jnp.int32, (LANES, LANES), 0)
    dst = lax.broadcasted_iota(jnp.int32, (LANES, LANES), 1)
    same_head = (src % FOX_HEADS == dst // BIAS_LANES) & (src < N_SPLIT * FOX_HEADS)
    place_q = jnp.where(same_head & (src // FOX_HEADS == dst % BIAS_LANES), 1.0, 0.0)
    place_k = jnp.where(same_head & (src // FOX_HEADS + N_SPLIT == dst % BIAS_LANES), 1.0, 0.0)
    pos = lane % BIAS_LANES
    ones_q = jnp.where((pos >= N_SPLIT) & (pos < 2 * N_SPLIT), 1.0, 0.0)
    ones_k = jnp.where(pos < N_SPLIT, 1.0, 0.0)
    qx_ref[...] = (_dot(packed, place_q.astype(BF16)) + ones_q).astype(BF16)
    kx_ref[...] = (ones_k - _dot(packed, place_k.astype(BF16))).astype(BF16)


def _fox_proj(x, mod, g, w, b_f, layer, normed):
    b, s, d = x.shape
    tm = TOKEN_TILE
    tok = lambda width: pl.BlockSpec((None, tm, width), lambda bi, si: (bi, si, 0))
    out_bf16 = lambda width: jax.ShapeDtypeStruct((b, s, width), BF16)
    return pl.pallas_call(
        functools.partial(_fox_proj_kernel, d=d, scale=(d // FOX_HEADS) ** -0.5, normed=normed),
        grid=(b, s // tm),
        in_specs=[
            tok(d),
            pl.BlockSpec((None, None, N_MOD, d), lambda bi, si: (layer, bi, 0, 0)),
            pl.BlockSpec((None, 1, d), lambda bi, si: (layer, 0, 0)),
            _resident(w.shape),
            _resident(b_f.shape),
        ],
        out_specs=[tok(d), tok(d),
                   pl.BlockSpec((None, d, tm), lambda bi, si: (bi, 0, si)),
                   tok(LANES), tok(LANES)],
        out_shape=[out_bf16(d), out_bf16(d),
                   jax.ShapeDtypeStruct((b, d, s), BF16),
                   out_bf16(LANES), out_bf16(LANES)],
        scratch_shapes=[pltpu.VMEM((1, LANES), F32)],
        compiler_params=_params(2),
        name="fox_proj",
    )(x, mod, g, w, b_f)


def _fox_attn_kernel(q_ref, qx_ref, k_ref, kx_ref, vt_ref, o_ref, s_ref, cmax_ref, p_ref):
    tq = ATTN_TILE
    hd = LANES // 2
    n_blocks = q_ref.shape[0] // tq
    pair = pl.program_id(1)
    lane = lax.broadcasted_iota(jnp.int32, (1, 2 * LANES), 1)
    key = lax.broadcasted_iota(jnp.int32, (tq, tq), 0)
    qry = lax.broadcasted_iota(jnp.int32, (tq, tq), 1)
    causal = key <= qry
    ones = jnp.ones((SUM_ROWS, tq), BF16)
    keep = []
    for hh in range(2):
        bias_lo = LANES + BIAS_LANES * (2 * pair + hh)
        keep.append(((lane >= hh * hd) & (lane < (hh + 1) * hd))
                    | ((lane >= bias_lo) & (lane < bias_lo + BIAS_LANES)))

    items = [(n, hh, j) for n in range(n_blocks) for j in [n] + list(range(n)) for hh in range(2)]

    def blk(i):
        return slice(i * tq, (i + 1) * tq)

    def scores(t):
        n, hh, j = items[t]
        q_full = jnp.concatenate([q_ref[blk(n), :], qx_ref[blk(n), :]], axis=1)
        qh = jnp.where(keep[hh], q_full, jnp.zeros_like(q_full))
        kb = jnp.concatenate([k_ref[blk(j), :], kx_ref[blk(j), :]], axis=1)
        s = _dot_nt(kb, qh)
        if j == n:
            s = jnp.where(causal, s, -jnp.inf)
        s_ref[t] = s
        cmax_ref[t] = jnp.max(s, axis=0, keepdims=True)

    run_max = {}
    rescale = {}

    def weights(t):
        n, hh, _ = items[t]
        m_old = run_max.get((n, hh))
        m_new = cmax_ref[t] if m_old is None else jnp.maximum(m_old, cmax_ref[t])
        p_ref[t] = jnp.exp2(s_ref[t] - m_new).astype(BF16)
        rescale[t] = None if m_old is None else jnp.exp2(m_old - m_new)
        run_max[(n, hh)] = m_new

    for t in range(PIPE_DEPTH):
        scores(t)
    weights(0)
    acc = {}
    for t, (n, hh, j) in enumerate(items):
        if t + PIPE_DEPTH < len(items):
            scores(t + PIPE_DEPTH)
        if t + 1 < len(items):
            weights(t + 1)
        vt = jnp.concatenate([vt_ref[hh * hd:(hh + 1) * hd, blk(j)], ones], axis=0)
        pv = _dot(vt, p_ref[t])
        acc[(n, hh)] = pv if rescale[t] is None else rescale[t] * acc[(n, hh)] + pv
        if j == (n - 1 if n else 0) and hh == 1:
            o_t = [acc[(n, h)][:hd, :] / acc[(n, h)][hd:hd + 1, :] for h in range(2)]
            o_ref[blk(n), :] = jnp.concatenate(o_t, axis=0).T.astype(BF16)


def _fox_attn(q, qx, k, kx, vt):
    b, s, d = q.shape
    tq = ATTN_TILE
    n_items = (s // tq) * (s // tq + 1)
    seq_blk = lambda col: pl.BlockSpec((None, s, LANES), col)
    return pl.pallas_call(
        _fox_attn_kernel,
        grid=(b, d // LANES),
        in_specs=[
            seq_blk(lambda bi, p: (bi, 0, p)),
            seq_blk(lambda bi, p: (bi, 0, 0)),
            seq_blk(lambda bi, p: (bi, 0, p)),
            seq_blk(lambda bi, p: (bi, 0, 0)),
            pl.BlockSpec((None, LANES, s), lambda bi, p: (bi, p, 0)),
        ],
        out_specs=seq_blk(lambda bi, p: (bi, 0, p)),
        out_shape=jax.ShapeDtypeStruct((b, s, d), BF16),
        scratch_shapes=[pltpu.VMEM((n_items, tq, tq), F32), pltpu.VMEM((n_items, 1, tq), F32),
                        pltpu.VMEM((n_items, tq, tq), BF16)],
        compiler_params=_params(2),
        name="fox_attn",
    )(q, qx, k, kx, vt)


def _gla_proj_kernel(x_ref, mod_ref, g_ref, w_ref, wa_ref, wa2_ref, ba_ref, go_ref,
                     qin_ref, kin_ref, kdec_ref, v_ref, gate_ref, dec_ref, b_ref,
                     *, dk, dv, scale, normed):
    tm = x_ref.shape[0]
    h = _input_h(x_ref, mod_ref, g_ref, normed)
    n_chunks = tm // CHUNK
    slab = 2 * dv // n_chunks

    def vr_slab(c):
        vr = _dot(h, w_ref[:, 2 * dk + c * slab:2 * dk + (c + 1) * slab])
        if (c + 1) * slab <= dv:
            v_ref[:, c * slab:(c + 1) * slab] = vr.astype(BF16)
        else:
            cols = slice(c * slab - dv, (c + 1) * slab - dv)
            gate_ref[:, cols] = _silu(vr) * go_ref[:, cols]

    a_lr = _dot(h, wa_ref[...]).astype(BF16)
    qk = _dot(h, w_ref[:, :2 * dk])
    log_alpha = _log_sigmoid(_dot(a_lr, wa2_ref[...]) + ba_ref[...]) / GLA_TAU
    for c in range(SLABS_AHEAD):
        vr_slab(c)

    row = lax.broadcasted_iota(jnp.int32, (CUM_BLOCK, CUM_BLOCK), 0)
    col = lax.broadcasted_iota(jnp.int32, (CUM_BLOCK, CUM_BLOCK), 1)
    tril = jnp.where((row >= col) & (row // CHUNK == col // CHUNK), 1.0, 0.0).astype(BF16)
    terms = jnp.concatenate(_split_bf16(log_alpha, GLA_SPLIT), axis=1)
    for i in range(tm // CUM_BLOCK):
        rows = slice(i * CUM_BLOCK, (i + 1) * CUM_BLOCK)
        sums = _dot(tril, terms[rows, :])
        b_ref[rows, :] = sum(sums[:, t * dk:(t + 1) * dk] for t in range(GLA_SPLIT))

    for c in range(n_chunks):
        if c + SLABS_AHEAD < n_chunks:
            vr_slab(c + SLABS_AHEAD)
        rows = slice(c * CHUNK, (c + 1) * CHUNK)
        b = b_ref[rows, :]
        b_last = b[CHUNK - 1:, :]
        k = qk[rows, dk:2 * dk]
        qin_ref[rows, :] = (qk[rows, :dk] * scale * jnp.exp(b)).astype(BF16)
        kin_ref[rows, :] = (k * jnp.exp(-b)).astype(BF16)
        kdec_ref[rows, :] = (k * jnp.exp(b_last - b)).astype(BF16)
        dec_ref[c:c + 1, :] = jnp.exp(b_last)


def _gla_proj(x, mod, g, w, wa, wa2, ba, g_o, layer, normed):
    b, s, d = x.shape
    tm = TOKEN_TILE
    dk = wa2.shape[1]
    dv = (w.shape[1] - 2 * dk) // 2
    tok = lambda width: pl.BlockSpec((None, tm, width), lambda bi, si: (bi, si, 0))
    out = lambda width, dt: jax.ShapeDtypeStruct((b, s, width), dt)
    return pl.pallas_call(
        functools.partial(_gla_proj_kernel, dk=dk, dv=dv, scale=(dk // GLA_HEADS) ** -0.5,
                          normed=normed),
        grid=(b, s // tm),
        in_specs=[
            tok(d),
            pl.BlockSpec((None, None, N_MOD, d), lambda bi, si: (layer, bi, 0, 0)),
            pl.BlockSpec((None, 1, d), lambda bi, si: (layer, 0, 0)),
            _resident(w.shape),
            _resident(wa.shape),
            _resident(wa2.shape),
            _resident(ba.shape),
            _resident(g_o.shape),
        ],
        out_specs=[tok(dk), tok(dk), tok(dk), tok(dv), tok(dv),
                   pl.BlockSpec((None, tm // CHUNK, dk), lambda bi, si: (bi, si, 0))],
        out_shape=[out(dk, BF16), out(dk, BF16), out(dk, BF16), out(dv, BF16), out(dv, F32),
                   jax.ShapeDtypeStruct((b, s // CHUNK, dk), F32)],
        scratch_shapes=[pltpu.VMEM((tm, dk), F32)],
        compiler_params=_params(2),
        name="gla_proj",
    )(x, mod, g, w, wa, wa2, ba, g_o)


def _gla_core_kernel(qin_ref, kin_ref, kdec_ref, v_ref, gate_ref, dec_ref,
                     o_ref, state_ref, kv_ref, a_ref):
    n_seq, tm, dk = qin_ref.shape
    hk = dk // GLA_HEADS
    hv = v_ref.shape[2] // GLA_HEADS

    @pl.when(pl.program_id(1) == 0)
    def _():
        state_ref[...] = jnp.zeros_like(state_ref)

    row = lax.broadcasted_iota(jnp.int32, (CHUNK, CHUNK), 0)
    col = lax.broadcasted_iota(jnp.int32, (CHUNK, CHUNK), 1)
    causal = row >= col

    items = [(c, hd, sq) for c in range(tm // CHUNK) for hd in range(GLA_HEADS)
             for sq in range(n_seq)]

    def span(c, hd):
        return (slice(c * CHUNK, (c + 1) * CHUNK), slice(hd * hk, (hd + 1) * hk),
                slice(hd * hv, (hd + 1) * hv))

    for i, (c, hd, sq) in enumerate(items):
        rows, kc, vc = span(c, hd)
        kv_ref[i] = _dot_tn(v_ref[sq, rows, vc], kdec_ref[sq, rows, kc])
        a = _dot_nt(qin_ref[sq, rows, kc], kin_ref[sq, rows, kc])
        a_ref[i] = jnp.where(causal, a, 0.0).astype(BF16)

    for i, (c, hd, sq) in enumerate(items):
        rows, kc, vc = span(c, hd)
        q = qin_ref[sq, rows, kc]
        state_t = state_ref[sq, hd]
        o = _dot(a_ref[i], v_ref[sq, rows, vc]) + _dot_nt(q, state_t.astype(BF16))
        state_ref[sq, hd] = state_t * dec_ref[sq, c:c + 1, kc] + kv_ref[i]
        o = o * lax.rsqrt(jnp.mean(o * o, axis=-1, keepdims=True) + EPS)
        o_ref[sq, rows, vc] = (o * gate_ref[sq, rows, vc]).astype(BF16)


def _gla_core(qin, kin, kdec, v, gate, dec):
    b, s, dk = qin.shape
    dv = v.shape[2]
    tm = TOKEN_TILE
    nb = GLA_BATCH
    n_items = nb * GLA_HEADS * (tm // CHUNK)
    tok = lambda width: pl.BlockSpec((nb, tm, width), lambda bi, si: (bi, si, 0))
    return pl.pallas_call(
        _gla_core_kernel,
        grid=(b // nb, s // tm),
        in_specs=[tok(dk), tok(dk), tok(dk), tok(dv), tok(dv),
                  pl.BlockSpec((nb, tm // CHUNK, dk), lambda bi, si: (bi, si, 0))],
        out_specs=tok(dv),
        out_shape=jax.ShapeDtypeStruct((b, s, dv), BF16),
        scratch_shapes=[pltpu.VMEM((nb, GLA_HEADS, dv // GLA_HEADS, dk // GLA_HEADS), F32),
                        pltpu.VMEM((n_items, dv // GLA_HEADS, dk // GLA_HEADS), F32),
                        pltpu.VMEM((n_items, CHUNK, CHUNK), BF16)],
        compiler_params=_params(2),
        name="gla_core",
    )(qin, kin, kdec, v, gate, dec)


def _mix_ffn_kernel(x_ref, o_ref, mod_ref, g_ref, wo_ref, win_ref, wout_ref, post_g_ref, post_mod_ref,
                    out_ref, *rest, hidden, final):
    acc_ref = rest[-1]
    tm = x_ref.shape[0]
    n_chunks = hidden // FFN_CHUNK
    parts = [slice(r * tm // ROW_PARTS, (r + 1) * tm // ROW_PARTS) for r in range(ROW_PARTS)]

    for rows in parts:
        out_ref[rows, :] = x_ref[rows, :] + mod_ref[2:3, :] * _dot(o_ref[rows, :], wo_ref[...])
    h = [_rms_mod(out_ref[rows, :], g_ref[...], mod_ref[3:4, :], mod_ref[4:5, :]).astype(BF16)
         for rows in parts]

    def gate_up(t):
        r, j = divmod(t, n_chunks)
        cols = slice(j * FFN_CHUNK, (j + 1) * FFN_CHUNK)
        up_cols = slice(hidden + j * FFN_CHUNK, hidden + (j + 1) * FFN_CHUNK)
        return (_silu(_dot(h[r], win_ref[:, cols])) * _dot(h[r], win_ref[:, up_cols])).astype(BF16)

    def epilogue(rows):
        y = out_ref[rows, :] + mod_ref[5:6, :] * acc_ref[rows, :]
        if final:
            y = y * lax.rsqrt(jnp.mean(y * y, axis=-1, keepdims=True) + EPS) * post_g_ref[...]
        else:
            rest[0][rows, :] = _rms_mod(
                y, post_g_ref[...], post_mod_ref[0:1, :], post_mod_ref[1:2, :]).astype(BF16)
        out_ref[rows, :] = y

    act = gate_up(0)
    for t in range(ROW_PARTS * n_chunks):
        nxt = gate_up(t + 1) if t + 1 < ROW_PARTS * n_chunks else None
        r, j = divmod(t, n_chunks)
        down = _dot(act, wout_ref[j * FFN_CHUNK:(j + 1) * FFN_CHUNK, :])
        if j == 0:
            acc_ref[parts[r], :] = down
        else:
            acc_ref[parts[r], :] += down
        if j == n_chunks - 1:
            epilogue(parts[r])
        act = nxt


def _mix_ffn(x, o, mod, g, wo, win, wout, post_g, layer, final):
    b, s, d = x.shape
    tm = TOKEN_TILE
    tok = pl.BlockSpec((None, tm, d), lambda bi, si: (bi, si, 0))
    nxt = layer if final else layer + 1
    post_g_spec = (_resident(post_g.shape) if final
                   else pl.BlockSpec((None, 1, d), lambda bi, si: (nxt, 0, 0)))
    outs = [jax.ShapeDtypeStruct((b, s, d), F32)] + ([] if final else [jax.ShapeDtypeStruct((b, s, d), BF16)])
    res = pl.pallas_call(
        functools.partial(_mix_ffn_kernel, hidden=wout.shape[0], final=final),
        grid=(b, s // tm),
        in_specs=[
            tok, tok,
            pl.BlockSpec((None, None, N_MOD, d), lambda bi, si: (layer, bi, 0, 0)),
            pl.BlockSpec((None, 1, d), lambda bi, si: (layer, 0, 0)),
            _resident(wo.shape), _resident(win.shape), _resident(wout.shape),
            post_g_spec,
            pl.BlockSpec((None, None, N_MOD, d), lambda bi, si: (nxt, bi, 0, 0)),
        ],
        out_specs=[tok] * len(outs),
        out_shape=outs,
        scratch_shapes=[pltpu.VMEM((tm, d), F32)],
        compiler_params=_params(2),
        name="mix_ffn",
    )(x, o, mod, g, wo, win, wout, post_g, mod)
    return (res[0], None) if final else (res[0], res[1])


def kernel(x, c, ada_w, ada_b, norm1_g, norm2_g, ffn_w_in, ffn_w_out, fox_w_in, fox_b_f, fox_w_out,
           gla_w_in, gla_w_a2, gla_b_a, gla_g_o, gla_w_out, final_g):
    depth, d, _ = ada_w.shape
    b = x.shape[0]
    assert x.shape[1] % TOKEN_TILE == 0 and TOKEN_TILE % ATTN_TILE == 0
    assert ffn_w_out.shape[1] % FFN_CHUNK == 0
    mod = _adaln_mod(c, ada_w, ada_b).reshape(depth, b, N_MOD, d)
    g1 = norm1_g.reshape(depth, 1, d)
    g2 = norm2_g.reshape(depth, 1, d)
    fg = final_g.reshape(1, d)
    h = None
    for i in range(depth):
        j = i // 2
        src, normed = (x, False) if h is None else (h, True)
        if i % 2 == 0:
            w = fox_w_in[j]
            w_f = w[:, 3 * d:]
            pad = jnp.zeros((d, LANES - N_SPLIT * FOX_HEADS), w.dtype)
            w = jnp.concatenate([w[:, :3 * d]] + [w_f] * N_SPLIT + [pad], axis=1).astype(BF16)
            b_f = jnp.concatenate(
                [fox_b_f[j]] * N_SPLIT + [jnp.zeros((LANES - N_SPLIT * FOX_HEADS,), F32)])
            q, k, vt, qx, kx = _fox_proj(src, mod, g1, w, b_f.reshape(1, LANES), i, normed)
            o = _fox_attn(q, qx, k, kx, vt)
            w_o = fox_w_out[j]
        else:
            w = gla_w_in[j]
            rank = gla_w_a2.shape[1]
            n_main = w.shape[1] - rank
            wa = jnp.pad(w[:, n_main:], ((0, 0), (0, LANES - rank))).astype(BF16)
            wa2 = jnp.pad(gla_w_a2[j], ((0, LANES - rank), (0, 0))).astype(BF16)
            qin, kin, kdec, v, gate, dec = _gla_proj(
                src, mod, g1, w[:, :n_main].astype(BF16), wa, wa2, gla_b_a[j].reshape(1, -1),
                gla_g_o[j].reshape(1, -1), i, normed)
            o = _gla_core(qin, kin, kdec, v, gate, dec)
            w_o = gla_w_out[j]
        final = i == depth - 1
        x, h = _mix_ffn(x, o, mod, g2, w_o.astype(BF16), ffn_w_in[i].astype(BF16),
                        ffn_w_out[i].astype(BF16), fg if final else g1, i, final)
    return x
```

```python
import functools
import math

import jax
import jax.numpy as jnp
from jax import lax
from jax.experimental import pallas as pl
from jax.experimental.pallas import tpu as pltpu

F32 = jnp.float32
BF16 = jnp.bfloat16

EPS = 1e-6
LOG2_E = math.log2(math.e)
N_MOD = 6
CHUNK = 64
FOX_HEADS = 16
GLA_HEADS = 4
GLA_TAU = 16.0
LANES = 128
BIAS_LANES = 8
N_SPLIT = 3
GLA_SPLIT = 2
SUM_ROWS = 16

TOKEN_TILE = 512
ATTN_TILE = 256
PIPE_DEPTH = 8
GLA_BATCH = 2
SLABS_AHEAD = 2
FFN_CHUNK = 256
ROW_PARTS = 2
CUM_BLOCK = 128
VMEM_LIMIT = 56 * 1024 * 1024


def _params(n_grid, flags=None):
    return pltpu.CompilerParams(
        dimension_semantics=("arbitrary",) * n_grid, vmem_limit_bytes=VMEM_LIMIT, flags=flags)


def _resident(shape):
    return pl.BlockSpec(shape, lambda *_: (0,) * len(shape), pipeline_mode=pl.Buffered(1))


def _layer_of(stack, idx):
    return pl.BlockSpec((None,) + stack.shape[1:], lambda *_: (idx, 0, 0),
                        pipeline_mode=pl.Buffered(1))


def _rms_mod(x, g, shift, scale):
    y = x * lax.rsqrt(jnp.mean(x * x, axis=-1, keepdims=True) + EPS) * g
    return y * (1.0 + scale) + shift


def _input_h(x_ref, mod_ref, g_ref, normed):
    if normed:
        return x_ref[...]
    return _rms_mod(x_ref[...], g_ref[...], mod_ref[0:1, :], mod_ref[1:2, :]).astype(BF16)


def _log_sigmoid(x):
    return jnp.minimum(x, 0.0) - jnp.log(1.0 + jnp.exp(-jnp.abs(x)))


def _silu(x):
    return x * jax.nn.sigmoid(x)


def _split_bf16(x, n_terms=N_SPLIT):
    terms = []
    for _ in range(n_terms - 1):
        t = x.astype(BF16)
        terms.append(t)
        x = x - t.astype(F32)
    terms.append(x.astype(BF16))
    return terms


def _dot(a, b):
    return jnp.dot(a, b, preferred_element_type=F32)


def _dot_nt(a, b):
    return lax.dot_general(a, b, (((1,), (1,)), ((), ())), preferred_element_type=F32)


def _dot_tn(a, b):
    return lax.dot_general(a, b, (((0,), (0,)), ((), ())), preferred_element_type=F32)


def _adaln_kernel(c_ref, w_ref, b_ref, o_ref):
    ca = _silu(c_ref[...]).astype(BF16)
    o_ref[...] = _dot(ca, w_ref[...].astype(BF16)) + b_ref[...]


def _adaln_mod(c, ada_w, ada_b):
    depth, d, n = ada_w.shape
    b = c.shape[0]
    bn = 1024
    return pl.pallas_call(
        _adaln_kernel,
        grid=(depth, n // bn),
        in_specs=[
            pl.BlockSpec((b, d), lambda i, j: (0, 0)),
            pl.BlockSpec((None, d, bn), lambda i, j: (i, 0, j)),
            pl.BlockSpec((None, 1, bn), lambda i, j: (i, 0, j)),
        ],
        out_specs=pl.BlockSpec((None, b, bn), lambda i, j: (i, 0, j)),
        out_shape=jax.ShapeDtypeStruct((depth, b, n), F32),
        compiler_params=_params(2),
        name="adaln_mod",
    )(c, ada_w, ada_b.reshape(depth, 1, n))


def _fox_proj_kernel(x_ref, mod_ref, g_ref, w_ref, bf_ref,
                     q_ref, k_ref, vt_ref, qx_ref, kx_ref, carry_ref, *, d, scale, normed):
    tm = x_ref.shape[0]
    part = tm // ROW_PARTS
    parts = [slice(r * part, (r + 1) * part) for r in range(ROW_PARTS)]

    @pl.when(pl.program_id(1) == 0)
    def _():
        carry_ref[...] = jnp.zeros_like(carry_ref)

    lane = lax.broadcasted_iota(jnp.int32, (1, LANES), 1)
    used = lane < N_SPLIT * FOX_HEADS
    group = lane // FOX_HEADS
    row = lax.broadcasted_iota(jnp.int32, (part, part), 0)
    col = lax.broadcasted_iota(jnp.int32, (part, part), 1)
    tril = jnp.where(row >= col, 1.0, 0.0).astype(BF16)
    src = lax.broadcasted_iota(jnp.int32, (LANES, LANES), 0)
    dst = lax.broadcasted_iota(jnp.int32, (LANES, LANES), 1)
    same_head = (src % FOX_HEADS == dst // BIAS_LANES) & (src < N_SPLIT * FOX_HEADS)
    place_q = jnp.where(same_head & (src // FOX_HEADS == dst % BIAS_LANES), 1.0, 0.0).astype(BF16)
    place_k = jnp.where(same_head & (src // FOX_HEADS + N_SPLIT == dst % BIAS_LANES), 1.0, 0.0).astype(BF16)
    pos = lane % BIAS_LANES
    ones_q = jnp.where((pos >= N_SPLIT) & (pos < 2 * N_SPLIT), 1.0, 0.0)
    ones_k = jnp.where(pos < N_SPLIT, 1.0, 0.0)

    h = [_input_h(x_ref.at[rows, :], mod_ref, g_ref, normed) for rows in parts]
    proj = [_dot(hp, w_ref[...]) for hp in h]
    log_f = []
    for rows, pr in zip(parts, proj):
        q_ref[rows, :] = (pr[:, :d] * (scale * LOG2_E)).astype(BF16)
        k_ref[rows, :] = pr[:, d:2 * d].astype(BF16)
        vt_ref[:, rows] = pr[:, 2 * d:3 * d].T.astype(BF16)
        log_f.append(jnp.where(used, _log_sigmoid(pr[:, 3 * d:] + bf_ref[...]), 0.0))
    sums = [_dot(tril, jnp.concatenate(_split_bf16(lf), axis=1)) for lf in log_f]
    carry = carry_ref[...]
    for rows, sm in zip(parts, sums):
        cum = carry + sum(sm[:, t * LANES:(t + 1) * LANES] for t in range(N_SPLIT))
        carry = cum[part - 1:part, :]
        terms = _split_bf16(cum * LOG2_E)
        packed = jnp.zeros_like(terms[0])
        for t in range(N_SPLIT):
            packed = jnp.where(group == t, terms[t], packed)
        qx_ref[rows, :] = (_dot(packed, place_q) + ones_q).astype(BF16)
        kx_ref[rows, :] = (ones_k - _dot(packed, place_k)).astype(BF16)
    carry_ref[...] = carry


def _fox_proj(x, mod, g, w, b_f, layer, normed):
    b, s, d = x.shape
    tm = TOKEN_TILE
    tok = lambda width: pl.BlockSpec((None, tm, width), lambda bi, si: (bi, si, 0))
    out_bf16 = lambda width: jax.ShapeDtypeStruct((b, s, width), BF16)
    return pl.pallas_call(
        functools.partial(_fox_proj_kernel, d=d, scale=(d // FOX_HEADS) ** -0.5, normed=normed),
        grid=(b, s // tm),
        in_specs=[
            tok(d),
            pl.BlockSpec((None, None, N_MOD, d), lambda bi, si: (layer, bi, 0, 0)),
            pl.BlockSpec((None, 1, d), lambda bi, si: (layer, 0, 0)),
            _resident(w.shape),
            _resident(b_f.shape),
        ],
        out_specs=[tok(d), tok(d),
                   pl.BlockSpec((None, d, tm), lambda bi, si: (bi, 0, si)),
                   tok(LANES), tok(LANES)],
        out_shape=[out_bf16(d), out_bf16(d),
                   jax.ShapeDtypeStruct((b, d, s), BF16),
                   out_bf16(LANES), out_bf16(LANES)],
        scratch_shapes=[pltpu.VMEM((1, LANES), F32)],
        compiler_params=_params(2),
        name="fox_proj",
    )(x, mod, g, w, b_f)


def _fox_attn_kernel(q_ref, qx_ref, k_ref, kx_ref, vt_ref, o_ref, s_ref, cmax_ref, p_ref):
    tq = ATTN_TILE
    hd = LANES // 2
    n_blocks = q_ref.shape[0] // tq
    pair = pl.program_id(1)
    lane = lax.broadcasted_iota(jnp.int32, (1, 2 * LANES), 1)
    key = lax.broadcasted_iota(jnp.int32, (tq, tq), 0)
    qry = lax.broadcasted_iota(jnp.int32, (tq, tq), 1)
    causal = key <= qry
    ones = jnp.ones((SUM_ROWS, tq), BF16)
    keep = []
    for hh in range(2):
        bias_lo = LANES + BIAS_LANES * (2 * pair + hh)
        keep.append(((lane >= hh * hd) & (lane < (hh + 1) * hd))
                    | ((lane >= bias_lo) & (lane < bias_lo + BIAS_LANES)))

    items = [(n, hh, j) for n in range(n_blocks) for j in [n] + list(range(n)) for hh in range(2)]

    def blk(i):
        return slice(i * tq, (i + 1) * tq)

    def scores(t):
        n, hh, j = items[t]
        q_full = jnp.concatenate([q_ref[blk(n), :], qx_ref[blk(n), :]], axis=1)
        qh = jnp.where(keep[hh], q_full, jnp.zeros_like(q_full))
        kb = jnp.concatenate([k_ref[blk(j), :], kx_ref[blk(j), :]], axis=1)
        s = _dot_nt(kb, qh)
        if j == n:
            s = jnp.where(causal, s, -jnp.inf)
        s_ref[t] = s
        cmax_ref[t] = jnp.max(s, axis=0, keepdims=True)

    run_max = {}
    rescale = {}

    def weights(t):
        n, hh, _ = items[t]
        m_old = run_max.get((n, hh))
        m_new = cmax_ref[t] if m_old is None else jnp.maximum(m_old, cmax_ref[t])
        p_ref[t] = jnp.exp2(s_ref[t] - m_new).astype(BF16)
        rescale[t] = None if m_old is None else jnp.exp2(m_old - m_new)
        run_max[(n, hh)] = m_new

    for t in range(PIPE_DEPTH):
        scores(t)
    weights(0)
    acc = {}
    for t, (n, hh, j) in enumerate(items):
        if t + PIPE_DEPTH < len(items):
            scores(t + PIPE_DEPTH)
        if t + 1 < len(items):
            weights(t + 1)
        vt = jnp.concatenate([vt_ref[hh * hd:(hh + 1) * hd, blk(j)], ones], axis=0)
        pv = _dot(vt, p_ref[t])
        acc[(n, hh)] = pv if rescale[t] is None else rescale[t] * acc[(n, hh)] + pv
        if j == (n - 1 if n else 0) and hh == 1:
            o_t = [acc[(n, h)][:hd, :] / acc[(n, h)][hd:hd + 1, :] for h in range(2)]
            o_ref[blk(n), :] = jnp.concatenate(o_t, axis=0).T.astype(BF16)


def _fox_attn(q, qx, k, kx, vt):
    b, s, d = q.shape
    tq = ATTN_TILE
    n_items = (s // tq) * (s // tq + 1)
    seq_blk = lambda col: pl.BlockSpec((None, s, LANES), col)
    return pl.pallas_call(
        _fox_attn_kernel,
        grid=(b, d // LANES),
        in_specs=[
            seq_blk(lambda bi, p: (bi, 0, p)),
            seq_blk(lambda bi, p: (bi, 0, 0)),
            seq_blk(lambda bi, p: (bi, 0, p)),
            seq_blk(lambda bi, p: (bi, 0, 0)),
            pl.BlockSpec((None, LANES, s), lambda bi, p: (bi, p, 0)),
        ],
        out_specs=seq_blk(lambda bi, p: (bi, 0, p)),
        out_shape=jax.ShapeDtypeStruct((b, s, d), BF16),
        scratch_shapes=[pltpu.VMEM((n_items, tq, tq), F32), pltpu.VMEM((n_items, 1, tq), F32),
                        pltpu.VMEM((n_items, tq, tq), BF16)],
        compiler_params=_params(2),
        name="fox_attn",
    )(q, qx, k, kx, vt)


def _gla_proj_kernel(x_ref, mod_ref, g_ref, w_ref, wa_ref, wa2_ref, ba_ref, go_ref,
                     qin_ref, kin_ref, kdec_ref, v_ref, gate_ref, dec_ref, b_ref,
                     *, dk, dv, scale, normed):
    tm = x_ref.shape[0]
    h = _input_h(x_ref, mod_ref, g_ref, normed)
    n_chunks = tm // CHUNK
    slab = 2 * dv // n_chunks

    def vr_slab(c):
        vr = _dot(h, w_ref[:, 2 * dk + c * slab:2 * dk + (c + 1) * slab])
        if (c + 1) * slab <= dv:
            v_ref[:, c * slab:(c + 1) * slab] = vr.astype(BF16)
        else:
            cols = slice(c * slab - dv, (c + 1) * slab - dv)
            gate_ref[:, cols] = _silu(vr) * go_ref[:, cols]

    a_lr = _dot(h, wa_ref[...]).astype(BF16)
    qk = _dot(h, w_ref[:, :2 * dk])
    log_alpha = _log_sigmoid(_dot(a_lr, wa2_ref[...]) + ba_ref[...]) / GLA_TAU
    for c in range(SLABS_AHEAD):
        vr_slab(c)

    row = lax.broadcasted_iota(jnp.int32, (CUM_BLOCK, CUM_BLOCK), 0)
    col = lax.broadcasted_iota(jnp.int32, (CUM_BLOCK, CUM_BLOCK), 1)
    tril = jnp.where((row >= col) & (row // CHUNK == col // CHUNK), 1.0, 0.0).astype(BF16)
    terms = jnp.concatenate(_split_bf16(log_alpha, GLA_SPLIT), axis=1)
    for i in range(tm // CUM_BLOCK):
        rows = slice(i * CUM_BLOCK, (i + 1) * CUM_BLOCK)
        sums = _dot(tril, terms[rows, :])
        b_ref[rows, :] = sum(sums[:, t * dk:(t + 1) * dk] for t in range(GLA_SPLIT))

    for c in range(n_chunks):
        if c + SLABS_AHEAD < n_chunks:
            vr_slab(c + SLABS_AHEAD)
        rows = slice(c * CHUNK, (c + 1) * CHUNK)
        b = b_ref[rows, :]
        b_last = b[CHUNK - 1:, :]
        k = qk[rows, dk:2 * dk]
        qin_ref[rows, :] = (qk[rows, :dk] * scale * jnp.exp(b)).astype(BF16)
        kin_ref[rows, :] = (k * jnp.exp(-b)).astype(BF16)
        kdec_ref[rows, :] = (k * jnp.exp(b_last - b)).astype(BF16)
        dec_ref[c:c + 1, :] = jnp.exp(b_last)


def _gla_proj(x, mod, g, w_stack, j, wa, wa2, ba, g_o, layer, normed):
    b, s, d = x.shape
    tm = TOKEN_TILE
    dk = wa2.shape[1]
    dv = g_o.shape[1]
    tok = lambda width: pl.BlockSpec((None, tm, width), lambda bi, si: (bi, si, 0))
    out = lambda width, dt: jax.ShapeDtypeStruct((b, s, width), dt)
    return pl.pallas_call(
        functools.partial(_gla_proj_kernel, dk=dk, dv=dv, scale=(dk // GLA_HEADS) ** -0.5,
                          normed=normed),
        grid=(b, s // tm),
        in_specs=[
            tok(d),
            pl.BlockSpec((None, None, N_MOD, d), lambda bi, si: (layer, bi, 0, 0)),
            pl.BlockSpec((None, 1, d), lambda bi, si: (layer, 0, 0)),
            _layer_of(w_stack, j),
            _resident(wa.shape),
            _resident(wa2.shape),
            _resident(ba.shape),
            _resident(g_o.shape),
        ],
        out_specs=[tok(dk), tok(dk), tok(dk), tok(dv), tok(dv),
                   pl.BlockSpec((None, tm // CHUNK, dk), lambda bi, si: (bi, si, 0))],
        out_shape=[out(dk, BF16), out(dk, BF16), out(dk, BF16), out(dv, BF16), out(dv, F32),
                   jax.ShapeDtypeStruct((b, s // CHUNK, dk), F32)],
        scratch_shapes=[pltpu.VMEM((tm, dk), F32)],
        compiler_params=_params(2),
        name="gla_proj",
    )(x, mod, g, w_stack, wa, wa2, ba, g_o)


def _gla_core_kernel(qin_ref, kin_ref, kdec_ref, v_ref, gate_ref, dec_ref,
                     o_ref, state_ref, kv_ref, a_ref):
    n_seq, tm, dk = qin_ref.shape
    hk = dk // GLA_HEADS
    hv = v_ref.shape[2] // GLA_HEADS

    @pl.when(pl.program_id(1) == 0)
    def _():
        state_ref[...] = jnp.zeros_like(state_ref)

    row = lax.broadcasted_iota(jnp.int32, (CHUNK, CHUNK), 0)
    col = lax.broadcasted_iota(jnp.int32, (CHUNK, CHUNK), 1)
    causal = row >= col

    items = [(c, hd, sq) for c in range(tm // CHUNK) for hd in range(GLA_HEADS)
             for sq in range(n_seq)]

    def span(c, hd):
        return (slice(c * CHUNK, (c + 1) * CHUNK), slice(hd * hk, (hd + 1) * hk),
                slice(hd * hv, (hd + 1) * hv))

    for i, (c, hd, sq) in enumerate(items):
        rows, kc, vc = span(c, hd)
        kv_ref[i] = _dot_tn(v_ref[sq, rows, vc], kdec_ref[sq, rows, kc])
        a = _dot_nt(qin_ref[sq, rows, kc], kin_ref[sq, rows, kc])
        a_ref[i] = jnp.where(causal, a, 0.0).astype(BF16)

    for i, (c, hd, sq) in enumerate(items):
        rows, kc, vc = span(c, hd)
        q = qin_ref[sq, rows, kc]
        state_t = state_ref[sq, hd]
        o = _dot(a_ref[i], v_ref[sq, rows, vc]) + _dot_nt(q, state_t.astype(BF16))
        state_ref[sq, hd] = state_t * dec_ref[sq, c:c + 1, kc] + kv_ref[i]
        o = o * lax.rsqrt(jnp.mean(o * o, axis=-1, keepdims=True) + EPS)
        o_ref[sq, rows, vc] = (o * gate_ref[sq, rows, vc]).astype(BF16)


def _gla_core(qin, kin, kdec, v, gate, dec):
    b, s, dk = qin.shape
    dv = v.shape[2]
    tm = TOKEN_TILE
    nb = GLA_BATCH
    n_items = nb * GLA_HEADS * (tm // CHUNK)
    tok = lambda width: pl.BlockSpec((nb, tm, width), lambda bi, si: (bi, si, 0))
    return pl.pallas_call(
        _gla_core_kernel,
        grid=(b // nb, s // tm),
        in_specs=[tok(dk), tok(dk), tok(dk), tok(dv), tok(dv),
                  pl.BlockSpec((nb, tm // CHUNK, dk), lambda bi, si: (bi, si, 0))],
        out_specs=tok(dv),
        out_shape=jax.ShapeDtypeStruct((b, s, dv), BF16),
        scratch_shapes=[pltpu.VMEM((nb, GLA_HEADS, dv // GLA_HEADS, dk // GLA_HEADS), F32),
                        pltpu.VMEM((n_items, dv // GLA_HEADS, dk // GLA_HEADS), F32),
                        pltpu.VMEM((n_items, CHUNK, CHUNK), BF16)],
        compiler_params=_params(2),
        name="gla_core",
    )(qin, kin, kdec, v, gate, dec)


def _mix_ffn_kernel(x_ref, o_ref, mod_ref, g_ref, wo_ref, win_ref, wout_ref, post_g_ref, post_mod_ref,
                    out_ref, *rest, hidden, final):
    acc_ref = rest[-1]
    tm = x_ref.shape[0]
    n_chunks = hidden // FFN_CHUNK
    parts = [slice(r * tm // ROW_PARTS, (r + 1) * tm // ROW_PARTS) for r in range(ROW_PARTS)]

    for rows in parts:
        out_ref[rows, :] = x_ref[rows, :] + mod_ref[2:3, :] * _dot(o_ref[rows, :], wo_ref[...])
    h = [_rms_mod(out_ref[rows, :], g_ref[...], mod_ref[3:4, :], mod_ref[4:5, :]).astype(BF16)
         for rows in parts]

    def gate_up(t):
        r, j = divmod(t, n_chunks)
        cols = slice(j * FFN_CHUNK, (j + 1) * FFN_CHUNK)
        up_cols = slice(hidden + j * FFN_CHUNK, hidden + (j + 1) * FFN_CHUNK)
        return (_silu(_dot(h[r], win_ref[:, cols])) * _dot(h[r], win_ref[:, up_cols])).astype(BF16)

    def epilogue(rows):
        y = out_ref[rows, :] + mod_ref[5:6, :] * acc_ref[rows, :]
        if final:
            y = y * lax.rsqrt(jnp.mean(y * y, axis=-1, keepdims=True) + EPS) * post_g_ref[...]
        else:
            rest[0][rows, :] = _rms_mod(
                y, post_g_ref[...], post_mod_ref[0:1, :], post_mod_ref[1:2, :]).astype(BF16)
        out_ref[rows, :] = y

    act = gate_up(0)
    for t in range(ROW_PARTS * n_chunks):
        nxt = gate_up(t + 1) if t + 1 < ROW_PARTS * n_chunks else None
        r, j = divmod(t, n_chunks)
        down = _dot(act, wout_ref[j * FFN_CHUNK:(j + 1) * FFN_CHUNK, :])
        if j == 0:
            acc_ref[parts[r], :] = down
        else:
            acc_ref[parts[r], :] += down
        if j == n_chunks - 1:
            epilogue(parts[r])
        act = nxt


def _mix_ffn(x, o, mod, g, wo_stack, j, win_stack, wout_stack, post_g, layer, final):
    b, s, d = x.shape
    tm = TOKEN_TILE
    tok = pl.BlockSpec((None, tm, d), lambda bi, si: (bi, si, 0))
    nxt = layer if final else layer + 1
    post_g_spec = (_resident(post_g.shape) if final
                   else pl.BlockSpec((None, 1, d), lambda bi, si: (nxt, 0, 0)))
    outs = [jax.ShapeDtypeStruct((b, s, d), F32)] + ([] if final else [jax.ShapeDtypeStruct((b, s, d), BF16)])
    res = pl.pallas_call(
        functools.partial(_mix_ffn_kernel, hidden=wout_stack.shape[1], final=final),
        grid=(b, s // tm),
        in_specs=[
            tok, tok,
            pl.BlockSpec((None, None, N_MOD, d), lambda bi, si: (layer, bi, 0, 0)),
            pl.BlockSpec((None, 1, d), lambda bi, si: (layer, 0, 0)),
            _layer_of(wo_stack, j), _layer_of(win_stack, layer), _layer_of(wout_stack, layer),
            post_g_spec,
            pl.BlockSpec((None, None, N_MOD, d), lambda bi, si: (nxt, bi, 0, 0)),
        ],
        out_specs=[tok] * len(outs),
        out_shape=outs,
        scratch_shapes=[pltpu.VMEM((tm, d), F32)],
        compiler_params=_params(2),
        name="mix_ffn",
    )(x, o, mod, g, wo_stack, win_stack, wout_stack, post_g, mod)
    return (res[0], None) if final else (res[0], res[1])


def kernel(x, c, ada_w, ada_b, norm1_g, norm2_g, ffn_w_in, ffn_w_out, fox_w_in, fox_b_f, fox_w_out,
           gla_w_in, gla_w_a2, gla_b_a, gla_g_o, gla_w_out, final_g):
    depth, d, _ = ada_w.shape
    b = x.shape[0]
    assert x.shape[1] % TOKEN_TILE == 0 and TOKEN_TILE % ATTN_TILE == 0
    assert ffn_w_out.shape[1] % FFN_CHUNK == 0
    mod = _adaln_mod(c, ada_w, ada_b).reshape(depth, b, N_MOD, d)
    g1 = norm1_g.reshape(depth, 1, d)
    g2 = norm2_g.reshape(depth, 1, d)
    fg = final_g.reshape(1, d)
    ffn_in, ffn_out = ffn_w_in.astype(BF16), ffn_w_out.astype(BF16)
    fox_in, fox_out = fox_w_in.astype(BF16), fox_w_out.astype(BF16)
    gla_in, gla_out = gla_w_in.astype(BF16), gla_w_out.astype(BF16)
    rank = gla_w_a2.shape[1]
    h = None
    for i in range(depth):
        j = i // 2
        src, normed = (x, False) if h is None else (h, True)
        if i % 2 == 0:
            w_f = fox_in[j][:, 3 * d:]
            pad = jnp.zeros((d, LANES - N_SPLIT * FOX_HEADS), BF16)
            w = jnp.concatenate([fox_in[j][:, :3 * d]] + [w_f] * N_SPLIT + [pad], axis=1)
            b_f = jnp.concatenate(
                [fox_b_f[j]] * N_SPLIT + [jnp.zeros((LANES - N_SPLIT * FOX_HEADS,), F32)])
            q, k, vt, qx, kx = _fox_proj(src, mod, g1, w, b_f.reshape(1, LANES), i, normed)
            o = _fox_attn(q, qx, k, kx, vt)
            w_o = fox_out
        else:
            n_main = gla_in.shape[2] - rank
            wa = jnp.pad(gla_in[j][:, n_main:], ((0, 0), (0, LANES - rank)))
            wa2 = jnp.pad(gla_w_a2[j], ((0, LANES - rank), (0, 0))).astype(BF16)
            qin, kin, kdec, v, gate, dec = _gla_proj(
                src, mod, g1, gla_in, j, wa, wa2, gla_b_a[j].reshape(1, -1),
                gla_g_o[j].reshape(1, -1), i, normed)
            o = _gla_core(qin, kin, kdec, v, gate, dec)
            w_o = gla_out
        final = i == depth - 1
        x, h = _mix_ffn(x, o, mod, g2, w_o, j, ffn_in, ffn_out, fg if final else g1, i, final)
    return x
```

```python
import functools
import math

import jax
import jax.numpy as jnp
from jax import lax
from jax.experimental import pallas as pl
from jax.experimental.pallas import tpu as pltpu

F32 = jnp.float32
BF16 = jnp.bfloat16

EPS = 1e-6
LOG2_E = math.log2(math.e)
N_MOD = 6
CHUNK = 64
FOX_HEADS = 16
GLA_HEADS = 4
GLA_TAU = 16.0
LANES = 128
BIAS_LANES = 8
N_SPLIT = 3
GLA_SPLIT = 2
SUM_ROWS = 16

TOKEN_TILE = 512
ATTN_TILE = 256
PIPE_DEPTH = 8
GLA_BATCH = 2
SLABS_AHEAD = 2
FFN_CHUNK = 256
ROW_PARTS = 2
CUM_BLOCK = 128
VMEM_LIMIT = 56 * 1024 * 1024


def _params(n_grid, flags=None):
    return pltpu.CompilerParams(
        dimension_semantics=("arbitrary",) * n_grid, vmem_limit_bytes=VMEM_LIMIT, flags=flags)


def _resident(shape):
    return pl.BlockSpec(shape, lambda *_: (0,) * len(shape), pipeline_mode=pl.Buffered(1))


def _layer_of(stack, idx):
    return pl.BlockSpec((None,) + stack.shape[1:], lambda *_: (idx, 0, 0),
                        pipeline_mode=pl.Buffered(1))


def _rms_mod(x, g, shift, scale):
    y = x * lax.rsqrt(jnp.mean(x * x, axis=-1, keepdims=True) + EPS) * g
    return y * (1.0 + scale) + shift


def _input_h(x_ref, mod_ref, g_ref, normed):
    if normed:
        return x_ref[...]
    return _rms_mod(x_ref[...], g_ref[...], mod_ref[0:1, :], mod_ref[1:2, :]).astype(BF16)


def _log_sigmoid(x):
    return jnp.minimum(x, 0.0) - jnp.log(1.0 + jnp.exp(-jnp.abs(x)))


def _silu(x):
    return x * jax.nn.sigmoid(x)


def _split_bf16(x, n_terms=N_SPLIT):
    terms = []
    for _ in range(n_terms - 1):
        t = x.astype(BF16)
        terms.append(t)
        x = x - t.astype(F32)
    terms.append(x.astype(BF16))
    return terms


def _dot(a, b):
    return jnp.dot(a, b, preferred_element_type=F32)


def _dot_nt(a, b):
    return lax.dot_general(a, b, (((1,), (1,)), ((), ())), preferred_element_type=F32)


def _dot_tn(a, b):
    return lax.dot_general(a, b, (((0,), (0,)), ((), ())), preferred_element_type=F32)


def _adaln_kernel(c_ref, w_ref, b_ref, o_ref):
    ca = _silu(c_ref[...]).astype(BF16)
    o_ref[...] = _dot(ca, w_ref[...].astype(BF16)) + b_ref[...]


def _adaln_mod(c, ada_w, ada_b):
    depth, d, n = ada_w.shape
    b = c.shape[0]
    bn = 1024
    return pl.pallas_call(
        _adaln_kernel,
        grid=(depth, n // bn),
        in_specs=[
            pl.BlockSpec((b, d), lambda i, j: (0, 0)),
            pl.BlockSpec((None, d, bn), lambda i, j: (i, 0, j)),
            pl.BlockSpec((None, 1, bn), lambda i, j: (i, 0, j)),
        ],
        out_specs=pl.BlockSpec((None, b, bn), lambda i, j: (i, 0, j)),
        out_shape=jax.ShapeDtypeStruct((depth, b, n), F32),
        compiler_params=_params(2),
        name="adaln_mod",
    )(c, ada_w, ada_b.reshape(depth, 1, n))


def _fox_proj_kernel(x_ref, mod_ref, g_ref, w_ref, bf_ref,
                     qt_ref, k_ref, vt_ref, qxt_ref, kx_ref, carry_ref, *, d, scale, normed):
    tm = x_ref.shape[0]
    part = tm // ROW_PARTS
    parts = [slice(r * part, (r + 1) * part) for r in range(ROW_PARTS)]

    @pl.when(pl.program_id(1) == 0)
    def _():
        carry_ref[...] = jnp.zeros_like(carry_ref)

    lane = lax.broadcasted_iota(jnp.int32, (1, LANES), 1)
    used = lane < N_SPLIT * FOX_HEADS
    group = lane // FOX_HEADS
    row = lax.broadcasted_iota(jnp.int32, (part, part), 0)
    col = lax.broadcasted_iota(jnp.int32, (part, part), 1)
    tril = jnp.where(row >= col, 1.0, 0.0).astype(BF16)
    src = lax.broadcasted_iota(jnp.int32, (LANES, LANES), 0)
    dst = lax.broadcasted_iota(jnp.int32, (LANES, LANES), 1)
    same_head = (src % FOX_HEADS == dst // BIAS_LANES) & (src < N_SPLIT * FOX_HEADS)
    place_q = jnp.where(same_head & (src // FOX_HEADS == dst % BIAS_LANES), 1.0, 0.0).astype(BF16)
    place_k = jnp.where(same_head & (src // FOX_HEADS + N_SPLIT == dst % BIAS_LANES), 1.0, 0.0).astype(BF16)
    pos = lane % BIAS_LANES
    ones_q = jnp.where((pos >= N_SPLIT) & (pos < 2 * N_SPLIT), 1.0, 0.0)
    ones_k = jnp.where(pos < N_SPLIT, 1.0, 0.0)

    h = [_input_h(x_ref.at[rows, :], mod_ref, g_ref, normed) for rows in parts]
    proj = [_dot(hp, w_ref[...]) for hp in h]
    log_f = []
    for rows, pr in zip(parts, proj):
        qt_ref[:, rows] = (pr[:, :d] * (scale * LOG2_E)).T.astype(BF16)
        k_ref[rows, :] = pr[:, d:2 * d].astype(BF16)
        vt_ref[:, rows] = pr[:, 2 * d:3 * d].T.astype(BF16)
        log_f.append(jnp.where(used, _log_sigmoid(pr[:, 3 * d:] + bf_ref[...]), 0.0))
    sums = [_dot(tril, jnp.concatenate(_split_bf16(lf), axis=1)) for lf in log_f]
    carry = carry_ref[...]
    for rows, sm in zip(parts, sums):
        cum = carry + sum(sm[:, t * LANES:(t + 1) * LANES] for t in range(N_SPLIT))
        carry = cum[part - 1:part, :]
        terms = _split_bf16(cum * LOG2_E)
        packed = jnp.zeros_like(terms[0])
        for t in range(N_SPLIT):
            packed = jnp.where(group == t, terms[t], packed)
        qxt_ref[:, rows] = (_dot(packed, place_q) + ones_q).T.astype(BF16)
        kx_ref[rows, :] = (ones_k - _dot(packed, place_k)).astype(BF16)
    carry_ref[...] = carry


def _fox_proj(x, mod, g, w, b_f, layer, normed):
    b, s, d = x.shape
    tm = TOKEN_TILE
    tok = lambda width: pl.BlockSpec((None, tm, width), lambda bi, si: (bi, si, 0))
    tr = lambda width: pl.BlockSpec((None, width, tm), lambda bi, si: (bi, 0, si))
    out_bf16 = lambda width: jax.ShapeDtypeStruct((b, s, width), BF16)
    out_t = lambda width: jax.ShapeDtypeStruct((b, width, s), BF16)
    return pl.pallas_call(
        functools.partial(_fox_proj_kernel, d=d, scale=(d // FOX_HEADS) ** -0.5, normed=normed),
        grid=(b, s // tm),
        in_specs=[
            tok(d),
            pl.BlockSpec((None, None, N_MOD, d), lambda bi, si: (layer, bi, 0, 0)),
            pl.BlockSpec((None, 1, d), lambda bi, si: (layer, 0, 0)),
            _resident(w.shape),
            _resident(b_f.shape),
        ],
        out_specs=[tr(d), tok(d), tr(d), tr(LANES), tok(LANES)],
        out_shape=[out_t(d), out_bf16(d), out_t(d), out_t(LANES), out_bf16(LANES)],
        scratch_shapes=[pltpu.VMEM((1, LANES), F32)],
        compiler_params=_params(2),
        name="fox_proj",
    )(x, mod, g, w, b_f)


def _fox_attn_kernel(qt_ref, qxt_ref, k_ref, kx_ref, vt_ref, o_ref, s_ref, cmax_ref, p_ref):
    tq = ATTN_TILE
    hd = LANES // 2
    n_blocks = k_ref.shape[0] // tq
    pair = pl.program_id(1)
    chan = lax.broadcasted_iota(jnp.int32, (2 * LANES, 1), 0)
    key = lax.broadcasted_iota(jnp.int32, (tq, tq), 0)
    qry = lax.broadcasted_iota(jnp.int32, (tq, tq), 1)
    causal = key <= qry
    ones = jnp.ones((SUM_ROWS, tq), BF16)
    keep = []
    for hh in range(2):
        bias_lo = LANES + BIAS_LANES * (2 * pair + hh)
        keep.append(((chan >= hh * hd) & (chan < (hh + 1) * hd))
                    | ((chan >= bias_lo) & (chan < bias_lo + BIAS_LANES)))

    items = [(n, hh, j) for n in range(n_blocks) for j in [n] + list(range(n)) for hh in range(2)]

    def blk(i):
        return slice(i * tq, (i + 1) * tq)

    head_q = {}

    def scores(t):
        n, hh, j = items[t]
        if (n, hh) not in head_q:
            q_full = jnp.concatenate([qt_ref[:, blk(n)], qxt_ref[:, blk(n)]], axis=0)
            head_q[(n, hh)] = jnp.where(keep[hh], q_full, jnp.zeros_like(q_full))
        kb = jnp.concatenate([k_ref[blk(j), :], kx_ref[blk(j), :]], axis=1)
        s = _dot(kb, head_q[(n, hh)])
        if j == n:
            s = jnp.where(causal, s, -jnp.inf)
        s_ref[t] = s
        cmax_ref[t] = jnp.max(s, axis=0, keepdims=True)

    run_max = {}
    rescale = {}

    def weights(t):
        n, hh, _ = items[t]
        m_old = run_max.get((n, hh))
        m_new = cmax_ref[t] if m_old is None else jnp.maximum(m_old, cmax_ref[t])
        p_ref[t] = jnp.exp2(s_ref[t] - m_new).astype(BF16)
        rescale[t] = None if m_old is None else jnp.exp2(m_old - m_new)
        run_max[(n, hh)] = m_new

    for t in range(PIPE_DEPTH):
        scores(t)
    weights(0)
    acc = {}
    for t, (n, hh, j) in enumerate(items):
        if t + PIPE_DEPTH < len(items):
            scores(t + PIPE_DEPTH)
        if t + 1 < len(items):
            weights(t + 1)
        vt = jnp.concatenate([vt_ref[hh * hd:(hh + 1) * hd, blk(j)], ones], axis=0)
        pv = _dot(vt, p_ref[t])
        acc[(n, hh)] = pv if rescale[t] is None else rescale[t] * acc[(n, hh)] + pv
        if j == (n - 1 if n else 0) and hh == 1:
            o_t = [acc[(n, h)][:hd, :] / acc[(n, h)][hd:hd + 1, :] for h in range(2)]
            o_ref[blk(n), :] = jnp.concatenate(o_t, axis=0).T.astype(BF16)


def _fox_attn(qt, qxt, k, kx, vt):
    b, s, d = k.shape
    tq = ATTN_TILE
    n_items = (s // tq) * (s // tq + 1)
    seq_blk = lambda col: pl.BlockSpec((None, s, LANES), col)
    chan_blk = lambda col: pl.BlockSpec((None, LANES, s), col)
    return pl.pallas_call(
        _fox_attn_kernel,
        grid=(b, d // LANES),
        in_specs=[
            chan_blk(lambda bi, p: (bi, p, 0)),
            chan_blk(lambda bi, p: (bi, 0, 0)),
            seq_blk(lambda bi, p: (bi, 0, p)),
            seq_blk(lambda bi, p: (bi, 0, 0)),
            chan_blk(lambda bi, p: (bi, p, 0)),
        ],
        out_specs=seq_blk(lambda bi, p: (bi, 0, p)),
        out_shape=jax.ShapeDtypeStruct((b, s, d), BF16),
        scratch_shapes=[pltpu.VMEM((n_items, tq, tq), F32), pltpu.VMEM((n_items, 1, tq), F32),
                        pltpu.VMEM((n_items, tq, tq), BF16)],
        compiler_params=_params(2),
        name="fox_attn",
    )(qt, qxt, k, kx, vt)


def _gla_proj_kernel(x_ref, mod_ref, g_ref, w_ref, wa_ref, wa2_ref, ba_ref, go_ref,
                     qin_ref, kin_ref, kdec_ref, v_ref, gate_ref, dec_ref, b_ref,
                     *, dk, dv, scale, normed):
    tm = x_ref.shape[0]
    h = _input_h(x_ref, mod_ref, g_ref, normed)
    n_chunks = tm // CHUNK
    slab = 2 * dv // n_chunks

    def vr_slab(c):
        vr = _dot(h, w_ref[:, 2 * dk + c * slab:2 * dk + (c + 1) * slab])
        if (c + 1) * slab <= dv:
            v_ref[:, c * slab:(c + 1) * slab] = vr.astype(BF16)
        else:
            cols = slice(c * slab - dv, (c + 1) * slab - dv)
            gate_ref[:, cols] = _silu(vr) * go_ref[:, cols]

    a_lr = _dot(h, wa_ref[...]).astype(BF16)
    qk = _dot(h, w_ref[:, :2 * dk])
    log_alpha = _log_sigmoid(_dot(a_lr, wa2_ref[...]) + ba_ref[...]) / GLA_TAU
    for c in range(SLABS_AHEAD):
        vr_slab(c)

    row = lax.broadcasted_iota(jnp.int32, (CUM_BLOCK, CUM_BLOCK), 0)
    col = lax.broadcasted_iota(jnp.int32, (CUM_BLOCK, CUM_BLOCK), 1)
    tril = jnp.where((row >= col) & (row // CHUNK == col // CHUNK), 1.0, 0.0).astype(BF16)
    terms = jnp.concatenate(_split_bf16(log_alpha, GLA_SPLIT), axis=1)
    for i in range(tm // CUM_BLOCK):
        rows = slice(i * CUM_BLOCK, (i + 1) * CUM_BLOCK)
        sums = _dot(tril, terms[rows, :])
        b_ref[rows, :] = sum(sums[:, t * dk:(t + 1) * dk] for t in range(GLA_SPLIT))

    for c in range(n_chunks):
        if c + SLABS_AHEAD < n_chunks:
            vr_slab(c + SLABS_AHEAD)
        rows = slice(c * CHUNK, (c + 1) * CHUNK)
        b = b_ref[rows, :]
        b_last = b[CHUNK - 1:, :]
        k = qk[rows, dk:2 * dk]
        qin_ref[rows, :] = (qk[rows, :dk] * scale * jnp.exp(b)).astype(BF16)
        kin_ref[rows, :] = (k * jnp.exp(-b)).astype(BF16)
        kdec_ref[rows, :] = (k * jnp.exp(b_last - b)).astype(BF16)
        dec_ref[c:c + 1, :] = jnp.exp(b_last)


def _gla_proj(x, mod, g, w_stack, j, wa, wa2, ba, g_o, layer, normed):
    b, s, d = x.shape
    tm = TOKEN_TILE
    dk = wa2.shape[1]
    dv = g_o.shape[1]
    tok = lambda width: pl.BlockSpec((None, tm, width), lambda bi, si: (bi, si, 0))
    out = lambda width, dt: jax.ShapeDtypeStruct((b, s, width), dt)
    return pl.pallas_call(
        functools.partial(_gla_proj_kernel, dk=dk, dv=dv, scale=(dk // GLA_HEADS) ** -0.5,
                          normed=normed),
        grid=(b, s // tm),
        in_specs=[
            tok(d),
            pl.BlockSpec((None, None, N_MOD, d), lambda bi, si: (layer, bi, 0, 0)),
            pl.BlockSpec((None, 1, d), lambda bi, si: (layer, 0, 0)),
            _layer_of(w_stack, j),
            _resident(wa.shape),
            _resident(wa2.shape),
            _resident(ba.shape),
            _resident(g_o.shape),
        ],
        out_specs=[tok(dk), tok(dk), tok(dk), tok(dv), tok(dv),
                   pl.BlockSpec((None, tm // CHUNK, dk), lambda bi, si: (bi, si, 0))],
        out_shape=[out(dk, BF16), out(dk, BF16), out(dk, BF16), out(dv, BF16), out(dv, F32),
                   jax.ShapeDtypeStruct((b, s // CHUNK, dk), F32)],
        scratch_shapes=[pltpu.VMEM((tm, dk), F32)],
        compiler_params=_params(2),
        name="gla_proj",
    )(x, mod, g, w_stack, wa, wa2, ba, g_o)


def _gla_core_kernel(qin_ref, kin_ref, kdec_ref, v_ref, gate_ref, dec_ref,
                     o_ref, state_ref, kv_ref, a_ref):
    n_seq, tm, dk = qin_ref.shape
    hk = dk // GLA_HEADS
    hv = v_ref.shape[2] // GLA_HEADS

    @pl.when(pl.program_id(1) == 0)
    def _():
        state_ref[...] = jnp.zeros_like(state_ref)

    row = lax.broadcasted_iota(jnp.int32, (CHUNK, CHUNK), 0)
    col = lax.broadcasted_iota(jnp.int32, (CHUNK, CHUNK), 1)
    causal = row >= col

    items = [(c, hd, sq) for c in range(tm // CHUNK) for hd in range(GLA_HEADS)
             for sq in range(n_seq)]

    def span(c, hd):
        return (slice(c * CHUNK, (c + 1) * CHUNK), slice(hd * hk, (hd + 1) * hk),
                slice(hd * hv, (hd + 1) * hv))

    for i, (c, hd, sq) in enumerate(items):
        rows, kc, vc = span(c, hd)
        kv_ref[i] = _dot_tn(v_ref[sq, rows, vc], kdec_ref[sq, rows, kc])
        a = _dot_nt(qin_ref[sq, rows, kc], kin_ref[sq, rows, kc])
        a_ref[i] = jnp.where(causal, a, 0.0).astype(BF16)

    for i, (c, hd, sq) in enumerate(items):
        rows, kc, vc = span(c, hd)
        q = qin_ref[sq, rows, kc]
        state_t = state_ref[sq, hd]
        o = _dot(a_ref[i], v_ref[sq, rows, vc]) + _dot_nt(q, state_t.astype(BF16))
        state_ref[sq, hd] = state_t * dec_ref[sq, c:c + 1, kc] + kv_ref[i]
        o = o * lax.rsqrt(jnp.mean(o * o, axis=-1, keepdims=True) + EPS)
        o_ref[sq, rows, vc] = (o * gate_ref[sq, rows, vc]).astype(BF16)


def _gla_core(qin, kin, kdec, v, gate, dec):
    b, s, dk = qin.shape
    dv = v.shape[2]
    tm = TOKEN_TILE
    nb = GLA_BATCH
    n_items = nb * GLA_HEADS * (tm // CHUNK)
    tok = lambda width: pl.BlockSpec((nb, tm, width), lambda bi, si: (bi, si, 0))
    return pl.pallas_call(
        _gla_core_kernel,
        grid=(b // nb, s // tm),
        in_specs=[tok(dk), tok(dk), tok(dk), tok(dv), tok(dv),
                  pl.BlockSpec((nb, tm // CHUNK, dk), lambda bi, si: (bi, si, 0))],
        out_specs=tok(dv),
        out_shape=jax.ShapeDtypeStruct((b, s, dv), BF16),
        scratch_shapes=[pltpu.VMEM((nb, GLA_HEADS, dv // GLA_HEADS, dk // GLA_HEADS), F32),
                        pltpu.VMEM((n_items, dv // GLA_HEADS, dk // GLA_HEADS), F32),
                        pltpu.VMEM((n_items, CHUNK, CHUNK), BF16)],
        compiler_params=_params(2),
        name="gla_core",
    )(qin, kin, kdec, v, gate, dec)


def _mix_ffn_kernel(x_ref, o_ref, mod_ref, g_ref, wo_ref, win_ref, wout_ref, post_g_ref, post_mod_ref,
                    out_ref, *rest, hidden, final):
    acc_ref = rest[-1]
    tm = x_ref.shape[0]
    n_chunks = hidden // FFN_CHUNK
    parts = [slice(r * tm // ROW_PARTS, (r + 1) * tm // ROW_PARTS) for r in range(ROW_PARTS)]

    for rows in parts:
        out_ref[rows, :] = x_ref[rows, :] + mod_ref[2:3, :] * _dot(o_ref[rows, :], wo_ref[...])
    h = [_rms_mod(out_ref[rows, :], g_ref[...], mod_ref[3:4, :], mod_ref[4:5, :]).astype(BF16)
         for rows in parts]

    def gate_up(t):
        r, j = divmod(t, n_chunks)
        cols = slice(j * FFN_CHUNK, (j + 1) * FFN_CHUNK)
        up_cols = slice(hidden + j * FFN_CHUNK, hidden + (j + 1) * FFN_CHUNK)
        return (_silu(_dot(h[r], win_ref[:, cols])) * _dot(h[r], win_ref[:, up_cols])).astype(BF16)

    def epilogue(rows):
        y = out_ref[rows, :] + mod_ref[5:6, :] * acc_ref[rows, :]
        if final:
            y = y * lax.rsqrt(jnp.mean(y * y, axis=-1, keepdims=True) + EPS) * post_g_ref[...]
        else:
            rest[0][rows, :] = _rms_mod(
                y, post_g_ref[...], post_mod_ref[0:1, :], post_mod_ref[1:2, :]).astype(BF16)
        out_ref[rows, :] = y

    act = gate_up(0)
    for t in range(ROW_PARTS * n_chunks):
        nxt = gate_up(t + 1) if t + 1 < ROW_PARTS * n_chunks else None
        r, j = divmod(t, n_chunks)
        down = _dot(act, wout_ref[j * FFN_CHUNK:(j + 1) * FFN_CHUNK, :])
        if j == 0:
            acc_ref[parts[r], :] = down
        else:
            acc_ref[parts[r], :] += down
        if j == n_chunks - 1:
            epilogue(parts[r])
        act = nxt


def _mix_ffn(x, o, mod, g, wo_stack, j, win_stack, wout_stack, post_g, layer, final):
    b, s, d = x.shape
    tm = TOKEN_TILE
    tok = pl.BlockSpec((None, tm, d), lambda bi, si: (bi, si, 0))
    nxt = layer if final else layer + 1
    post_g_spec = (_resident(post_g.shape) if final
                   else pl.BlockSpec((None, 1, d), lambda bi, si: (nxt, 0, 0)))
    outs = [jax.ShapeDtypeStruct((b, s, d), F32)] + ([] if final else [jax.ShapeDtypeStruct((b, s, d), BF16)])
    res = pl.pallas_call(
        functools.partial(_mix_ffn_kernel, hidden=wout_stack.shape[1], final=final),
        grid=(b, s // tm),
        in_specs=[
            tok, tok,
            pl.BlockSpec((None, None, N_MOD, d), lambda bi, si: (layer, bi, 0, 0)),
            pl.BlockSpec((None, 1, d), lambda bi, si: (layer, 0, 0)),
            _layer_of(wo_stack, j), _layer_of(win_stack, layer), _layer_of(wout_stack, layer),
            post_g_spec,
            pl.BlockSpec((None, None, N_MOD, d), lambda bi, si: (nxt, bi, 0, 0)),
        ],
        out_specs=[tok] * len(outs),
        out_shape=outs,
        scratch_shapes=[pltpu.VMEM((tm, d), F32)],
        compiler_params=_params(2),
        name="mix_ffn",
    )(x, o, mod, g, wo_stack, win_stack, wout_stack, post_g, mod)
    return (res[0], None) if final else (res[0], res[1])


def kernel(x, c, ada_w, ada_b, norm1_g, norm2_g, ffn_w_in, ffn_w_out, fox_w_in, fox_b_f, fox_w_out,
           gla_w_in, gla_w_a2, gla_b_a, gla_g_o, gla_w_out, final_g):
    depth, d, _ = ada_w.shape
    b = x.shape[0]
    assert x.shape[1] % TOKEN_TILE == 0 and TOKEN_TILE % ATTN_TILE == 0
    assert ffn_w_out.shape[1] % FFN_CHUNK == 0
    mod = _adaln_mod(c, ada_w, ada_b).reshape(depth, b, N_MOD, d)
    g1 = norm1_g.reshape(depth, 1, d)
    g2 = norm2_g.reshape(depth, 1, d)
    fg = final_g.reshape(1, d)
    ffn_in, ffn_out = ffn_w_in.astype(BF16), ffn_w_out.astype(BF16)
    fox_in, fox_out = fox_w_in.astype(BF16), fox_w_out.astype(BF16)
    gla_in, gla_out = gla_w_in.astype(BF16), gla_w_out.astype(BF16)
    rank = gla_w_a2.shape[1]
    h = None
    for i in range(depth):
        j = i // 2
        src, normed = (x, False) if h is None else (h, True)
        if i % 2 == 0:
            w_f = fox_in[j][:, 3 * d:]
            pad = jnp.zeros((d, LANES - N_SPLIT * FOX_HEADS), BF16)
            w = jnp.concatenate([fox_in[j][:, :3 * d]] + [w_f] * N_SPLIT + [pad], axis=1)
            b_f = jnp.concatenate(
                [fox_b_f[j]] * N_SPLIT + [jnp.zeros((LANES - N_SPLIT * FOX_HEADS,), F32)])
            qt, k, vt, qxt, kx = _fox_proj(src, mod, g1, w, b_f.reshape(1, LANES), i, normed)
            o = _fox_attn(qt, qxt, k, kx, vt)
            w_o = fox_out
        else:
            n_main = gla_in.shape[2] - rank
            wa = jnp.pad(gla_in[j][:, n_main:], ((0, 0), (0, LANES - rank)))
            wa2 = jnp.pad(gla_w_a2[j], ((0, LANES - rank), (0, 0))).astype(BF16)
            qin, kin, kdec, v, gate, dec = _gla_proj(
                src, mod, g1, gla_in, j, wa, wa2, gla_b_a[j].reshape(1, -1),
                gla_g_o[j].reshape(1, -1), i, normed)
            o = _gla_core(qin, kin, kdec, v, gate, dec)
            w_o = gla_out
        final = i == depth - 1
        x, h = _mix_ffn(x, o, mod, g2, w_o, j, ffn_in, ffn_out, fg if final else g1, i, final)
    return x
```

```python
import functools
import math

import jax
import jax.numpy as jnp
from jax import lax
from jax.experimental import pallas as pl
from jax.experimental.pallas import tpu as pltpu

F32 = jnp.float32
BF16 = jnp.bfloat16

EPS = 1e-6
LOG2_E = math.log2(math.e)
N_MOD = 6
CHUNK = 64
FOX_HEADS = 16
GLA_HEADS = 4
GLA_TAU = 16.0
LANES = 128
BIAS_LANES = 8
N_SPLIT = 3
GLA_SPLIT = 2
SUM_ROWS = 16

TOKEN_TILE = 512
ATTN_TILE = 256
PIPE_DEPTH = 6
GLA_BATCH = 2
SLABS_AHEAD = 3
FFN_CHUNK = 256
ROW_PARTS = 2
CUM_BLOCK = 128
VMEM_LIMIT = 56 * 1024 * 1024


def _params(n_grid, flags=None):
    return pltpu.CompilerParams(
        dimension_semantics=("arbitrary",) * n_grid, vmem_limit_bytes=VMEM_LIMIT, flags=flags)


def _resident(shape):
    return pl.BlockSpec(shape, lambda *_: (0,) * len(shape), pipeline_mode=pl.Buffered(1))


def _layer_of(stack, idx):
    return pl.BlockSpec((None,) + stack.shape[1:], lambda *_: (idx, 0, 0),
                        pipeline_mode=pl.Buffered(1))


def _rms_mod(x, g, shift, scale):
    y = x * lax.rsqrt(jnp.mean(x * x, axis=-1, keepdims=True) + EPS) * g
    return y * (1.0 + scale) + shift


def _input_h(x_ref, mod_ref, g_ref, normed):
    if normed:
        return x_ref[...]
    return _rms_mod(x_ref[...], g_ref[...], mod_ref[0:1, :], mod_ref[1:2, :]).astype(BF16)


def _log_sigmoid(x):
    return jnp.minimum(x, 0.0) - jnp.log(1.0 + jnp.exp(-jnp.abs(x)))


def _silu(x):
    return x * jax.nn.sigmoid(x)


def _split_bf16(x, n_terms=N_SPLIT):
    terms = []
    for _ in range(n_terms - 1):
        t = x.astype(BF16)
        terms.append(t)
        x = x - t.astype(F32)
    terms.append(x.astype(BF16))
    return terms


def _dot(a, b):
    return jnp.dot(a, b, preferred_element_type=F32)


def _dot_nt(a, b):
    return lax.dot_general(a, b, (((1,), (1,)), ((), ())), preferred_element_type=F32)


def _dot_tn(a, b):
    return lax.dot_general(a, b, (((0,), (0,)), ((), ())), preferred_element_type=F32)


def _adaln_kernel(c_ref, w_ref, b_ref, o_ref):
    ca = _silu(c_ref[...]).astype(BF16)
    o_ref[...] = _dot(ca, w_ref[...].astype(BF16)) + b_ref[...]


def _adaln_mod(c, ada_w, ada_b):
    depth, d, n = ada_w.shape
    b = c.shape[0]
    bn = 1024
    return pl.pallas_call(
        _adaln_kernel,
        grid=(depth, n // bn),
        in_specs=[
            pl.BlockSpec((b, d), lambda i, j: (0, 0)),
            pl.BlockSpec((None, d, bn), lambda i, j: (i, 0, j)),
            pl.BlockSpec((None, 1, bn), lambda i, j: (i, 0, j)),
        ],
        out_specs=pl.BlockSpec((None, b, bn), lambda i, j: (i, 0, j)),
        out_shape=jax.ShapeDtypeStruct((depth, b, n), F32),
        compiler_params=_params(2),
        name="adaln_mod",
    )(c, ada_w, ada_b.reshape(depth, 1, n))


def _fox_proj_kernel(x_ref, mod_ref, g_ref, w_ref, bf_ref,
                     qt_ref, k_ref, vt_ref, qxt_ref, kx_ref, carry_ref, *, d, scale, normed):
    tm = x_ref.shape[0]
    part = tm // ROW_PARTS
    parts = [slice(r * part, (r + 1) * part) for r in range(ROW_PARTS)]

    @pl.when(pl.program_id(1) == 0)
    def _():
        carry_ref[...] = jnp.zeros_like(carry_ref)

    lane = lax.broadcasted_iota(jnp.int32, (1, LANES), 1)
    used = lane < N_SPLIT * FOX_HEADS
    group = lane // FOX_HEADS
    row = lax.broadcasted_iota(jnp.int32, (part, part), 0)
    col = lax.broadcasted_iota(jnp.int32, (part, part), 1)
    tril = jnp.where(row >= col, 1.0, 0.0).astype(BF16)
    src = lax.broadcasted_iota(jnp.int32, (LANES, LANES), 0)
    dst = lax.broadcasted_iota(jnp.int32, (LANES, LANES), 1)
    same_head = (src % FOX_HEADS == dst // BIAS_LANES) & (src < N_SPLIT * FOX_HEADS)
    place_q = jnp.where(same_head & (src // FOX_HEADS == dst % BIAS_LANES), 1.0, 0.0).astype(BF16)
    place_k = jnp.where(same_head & (src // FOX_HEADS + N_SPLIT == dst % BIAS_LANES), 1.0, 0.0).astype(BF16)
    pos = lane % BIAS_LANES
    ones_q = jnp.where((pos >= N_SPLIT) & (pos < 2 * N_SPLIT), 1.0, 0.0)
    ones_k = jnp.where(pos < N_SPLIT, 1.0, 0.0)

    h = [_input_h(x_ref.at[rows, :], mod_ref, g_ref, normed) for rows in parts]
    proj = [_dot(hp, w_ref[...]) for hp in h]
    log_f = []
    for rows, pr in zip(parts, proj):
        qt_ref[:, rows] = (pr[:, :d] * (scale * LOG2_E)).T.astype(BF16)
        k_ref[rows, :] = pr[:, d:2 * d].astype(BF16)
        vt_ref[:, rows] = pr[:, 2 * d:3 * d].T.astype(BF16)
        log_f.append(jnp.where(used, _log_sigmoid(pr[:, 3 * d:] + bf_ref[...]), 0.0))
    sums = [_dot(tril, jnp.concatenate(_split_bf16(lf), axis=1)) for lf in log_f]
    carry = carry_ref[...]
    for rows, sm in zip(parts, sums):
        cum = carry + sum(sm[:, t * LANES:(t + 1) * LANES] for t in range(N_SPLIT))
        carry = cum[part - 1:part, :]
        terms = _split_bf16(cum * LOG2_E)
        packed = jnp.zeros_like(terms[0])
        for t in range(N_SPLIT):
            packed = jnp.where(group == t, terms[t], packed)
        qxt_ref[:, rows] = (_dot(packed, place_q) + ones_q).T.astype(BF16)
        kx_ref[rows, :] = (ones_k - _dot(packed, place_k)).astype(BF16)
    carry_ref[...] = carry


def _fox_proj(x, mod, g, w, b_f, layer, normed):
    b, s, d = x.shape
    tm = TOKEN_TILE
    tok = lambda width: pl.BlockSpec((None, tm, width), lambda bi, si: (bi, si, 0))
    tr = lambda width: pl.BlockSpec((None, width, tm), lambda bi, si: (bi, 0, si))
    out_bf16 = lambda width: jax.ShapeDtypeStruct((b, s, width), BF16)
    out_t = lambda width: jax.ShapeDtypeStruct((b, width, s), BF16)
    return pl.pallas_call(
        functools.partial(_fox_proj_kernel, d=d, scale=(d // FOX_HEADS) ** -0.5, normed=normed),
        grid=(b, s // tm),
        in_specs=[
            tok(d),
            pl.BlockSpec((None, None, N_MOD, d), lambda bi, si: (layer, bi, 0, 0)),
            pl.BlockSpec((None, 1, d), lambda bi, si: (layer, 0, 0)),
            _resident(w.shape),
            _resident(b_f.shape),
        ],
        out_specs=[tr(d), tok(d), tr(d), tr(LANES), tok(LANES)],
        out_shape=[out_t(d), out_bf16(d), out_t(d), out_t(LANES), out_bf16(LANES)],
        scratch_shapes=[pltpu.VMEM((1, LANES), F32)],
        compiler_params=_params(2),
        name="fox_proj",
    )(x, mod, g, w, b_f)


def _fox_attn_kernel(qt_ref, qxt_ref, k_ref, kx_ref, vt_ref, o_ref, s_ref, cmax_ref, p_ref):
    tq = ATTN_TILE
    hd = LANES // 2
    n_blocks = k_ref.shape[0] // tq
    pair = pl.program_id(1)
    chan = lax.broadcasted_iota(jnp.int32, (2 * LANES, 1), 0)
    key = lax.broadcasted_iota(jnp.int32, (tq, tq), 0)
    qry = lax.broadcasted_iota(jnp.int32, (tq, tq), 1)
    causal = key <= qry
    ones = jnp.ones((SUM_ROWS, tq), BF16)
    keep = []
    for hh in range(2):
        bias_lo = LANES + BIAS_LANES * (2 * pair + hh)
        keep.append(((chan >= hh * hd) & (chan < (hh + 1) * hd))
                    | ((chan >= bias_lo) & (chan < bias_lo + BIAS_LANES)))

    items = [(n, hh, j) for n in range(n_blocks) for j in [n] + list(range(n)) for hh in range(2)]

    def blk(i):
        return slice(i * tq, (i + 1) * tq)

    head_q = {}

    def scores(t):
        n, hh, j = items[t]
        if (n, hh) not in head_q:
            q_full = jnp.concatenate([qt_ref[:, blk(n)], qxt_ref[:, blk(n)]], axis=0)
            head_q[(n, hh)] = jnp.where(keep[hh], q_full, jnp.zeros_like(q_full))
        kb = jnp.concatenate([k_ref[blk(j), :], kx_ref[blk(j), :]], axis=1)
        s = _dot(kb, head_q[(n, hh)])
        if j == n:
            s = jnp.where(causal, s, -jnp.inf)
        s_ref[t] = s
        cmax_ref[t] = jnp.max(s, axis=0, keepdims=True)

    run_max = {}
    rescale = {}

    def weights(t):
        n, hh, _ = items[t]
        m_old = run_max.get((n, hh))
        m_new = cmax_ref[t] if m_old is None else jnp.maximum(m_old, cmax_ref[t])
        p_ref[t] = jnp.exp2(s_ref[t] - m_new).astype(BF16)
        rescale[t] = None if m_old is None else jnp.exp2(m_old - m_new)
        run_max[(n, hh)] = m_new

    for t in range(PIPE_DEPTH):
        scores(t)
    weights(0)
    acc = {}
    for t, (n, hh, j) in enumerate(items):
        if t + PIPE_DEPTH < len(items):
            scores(t + PIPE_DEPTH)
        if t + 1 < len(items):
            weights(t + 1)
        vt = jnp.concatenate([vt_ref[hh * hd:(hh + 1) * hd, blk(j)], ones], axis=0)
        pv = _dot(vt, p_ref[t])
        acc[(n, hh)] = pv if rescale[t] is None else rescale[t] * acc[(n, hh)] + pv
        if j == (n - 1 if n else 0) and hh == 1:
            o_t = [acc[(n, h)][:hd, :] / acc[(n, h)][hd:hd + 1, :] for h in range(2)]
            o_ref[blk(n), :] = jnp.concatenate(o_t, axis=0).T.astype(BF16)


def _fox_attn(qt, qxt, k, kx, vt):
    b, s, d = k.shape
    tq = ATTN_TILE
    n_items = (s // tq) * (s // tq + 1)
    seq_blk = lambda col: pl.BlockSpec((None, s, LANES), col)
    chan_blk = lambda col: pl.BlockSpec((None, LANES, s), col)
    return pl.pallas_call(
        _fox_attn_kernel,
        grid=(b, d // LANES),
        in_specs=[
            chan_blk(lambda bi, p: (bi, p, 0)),
            chan_blk(lambda bi, p: (bi, 0, 0)),
            seq_blk(lambda bi, p: (bi, 0, p)),
            seq_blk(lambda bi, p: (bi, 0, 0)),
            chan_blk(lambda bi, p: (bi, p, 0)),
        ],
        out_specs=seq_blk(lambda bi, p: (bi, 0, p)),
        out_shape=jax.ShapeDtypeStruct((b, s, d), BF16),
        scratch_shapes=[pltpu.VMEM((n_items, tq, tq), F32), pltpu.VMEM((n_items, 1, tq), F32),
                        pltpu.VMEM((n_items, tq, tq), BF16)],
        compiler_params=_params(2),
        name="fox_attn",
    )(qt, qxt, k, kx, vt)


def _gla_proj_kernel(x_ref, mod_ref, g_ref, w_ref, wa_ref, wa2_ref, ba_ref, go_ref,
                     qin_ref, kin_ref, kdec_ref, v_ref, gate_ref, dec_ref, b_ref,
                     *, dk, dv, scale, normed):
    tm = x_ref.shape[0]
    h = _input_h(x_ref, mod_ref, g_ref, normed)
    n_chunks = tm // CHUNK
    slab = 2 * dv // n_chunks

    def vr_slab(c):
        vr = _dot(h, w_ref[:, 2 * dk + c * slab:2 * dk + (c + 1) * slab])
        if (c + 1) * slab <= dv:
            v_ref[:, c * slab:(c + 1) * slab] = vr.astype(BF16)
        else:
            cols = slice(c * slab - dv, (c + 1) * slab - dv)
            gate_ref[:, cols] = _silu(vr) * go_ref[:, cols]

    a_lr = _dot(h, wa_ref[...]).astype(BF16)
    qk = _dot(h, w_ref[:, :2 * dk])
    log_alpha = _log_sigmoid(_dot(a_lr, wa2_ref[...]) + ba_ref[...]) / GLA_TAU
    for c in range(SLABS_AHEAD):
        vr_slab(c)

    row = lax.broadcasted_iota(jnp.int32, (CUM_BLOCK, CUM_BLOCK), 0)
    col = lax.broadcasted_iota(jnp.int32, (CUM_BLOCK, CUM_BLOCK), 1)
    tril = jnp.where((row >= col) & (row // CHUNK == col // CHUNK), 1.0, 0.0).astype(BF16)
    terms = jnp.concatenate(_split_bf16(log_alpha, GLA_SPLIT), axis=1)
    for i in range(tm // CUM_BLOCK):
        rows = slice(i * CUM_BLOCK, (i + 1) * CUM_BLOCK)
        sums = _dot(tril, terms[rows, :])
        b_ref[rows, :] = sum(sums[:, t * dk:(t + 1) * dk] for t in range(GLA_SPLIT))

    for c in range(n_chunks):
        if c + SLABS_AHEAD < n_chunks:
            vr_slab(c + SLABS_AHEAD)
        rows = slice(c * CHUNK, (c + 1) * CHUNK)
        b = b_ref[rows, :]
        b_last = b[CHUNK - 1:, :]
        k = qk[rows, dk:2 * dk]
        qin_ref[rows, :] = (qk[rows, :dk] * scale * jnp.exp(b)).astype(BF16)
        kin_ref[rows, :] = (k * jnp.exp(-b)).astype(BF16)
        kdec_ref[rows, :] = (k * jnp.exp(b_last - b)).astype(BF16)
        dec_ref[c:c + 1, :] = jnp.exp(b_last)


def _gla_proj(x, mod, g, w_stack, j, wa, wa2, ba, g_o, layer, normed):
    b, s, d = x.shape
    tm = TOKEN_TILE
    dk = wa2.shape[1]
    dv = g_o.shape[1]
    tok = lambda width: pl.BlockSpec((None, tm, width), lambda bi, si: (bi, si, 0))
    out = lambda width, dt: jax.ShapeDtypeStruct((b, s, width), dt)
    return pl.pallas_call(
        functools.partial(_gla_proj_kernel, dk=dk, dv=dv, scale=(dk // GLA_HEADS) ** -0.5,
                          normed=normed),
        grid=(b, s // tm),
        in_specs=[
            tok(d),
            pl.BlockSpec((None, None, N_MOD, d), lambda bi, si: (layer, bi, 0, 0)),
            pl.BlockSpec((None, 1, d), lambda bi, si: (layer, 0, 0)),
            _layer_of(w_stack, j),
            _resident(wa.shape),
            _resident(wa2.shape),
            _resident(ba.shape),
            _resident(g_o.shape),
        ],
        out_specs=[tok(dk), tok(dk), tok(dk), tok(dv), tok(dv),
                   pl.BlockSpec((None, tm // CHUNK, dk), lambda bi, si: (bi, si, 0))],
        out_shape=[out(dk, BF16), out(dk, BF16), out(dk, BF16), out(dv, BF16), out(dv, F32),
                   jax.ShapeDtypeStruct((b, s // CHUNK, dk), F32)],
        scratch_shapes=[pltpu.VMEM((tm, dk), F32)],
        compiler_params=_params(2),
        name="gla_proj",
    )(x, mod, g, w_stack, wa, wa2, ba, g_o)


def _gla_core_kernel(qin_ref, kin_ref, kdec_ref, v_ref, gate_ref, dec_ref,
                     o_ref, state_ref, kv_ref, a_ref):
    n_seq, tm, dk = qin_ref.shape
    hk = dk // GLA_HEADS
    hv = v_ref.shape[2] // GLA_HEADS

    @pl.when(pl.program_id(1) == 0)
    def _():
        state_ref[...] = jnp.zeros_like(state_ref)

    row = lax.broadcasted_iota(jnp.int32, (CHUNK, CHUNK), 0)
    col = lax.broadcasted_iota(jnp.int32, (CHUNK, CHUNK), 1)
    causal = row >= col

    items = [(c, hd, sq) for c in range(tm // CHUNK) for hd in range(GLA_HEADS)
             for sq in range(n_seq)]

    def span(c, hd):
        return (slice(c * CHUNK, (c + 1) * CHUNK), slice(hd * hk, (hd + 1) * hk),
                slice(hd * hv, (hd + 1) * hv))

    for i, (c, hd, sq) in enumerate(items):
        rows, kc, vc = span(c, hd)
        kv_ref[i] = _dot_tn(v_ref[sq, rows, vc], kdec_ref[sq, rows, kc])
        a = _dot_nt(qin_ref[sq, rows, kc], kin_ref[sq, rows, kc])
        a_ref[i] = jnp.where(causal, a, 0.0).astype(BF16)

    for i, (c, hd, sq) in enumerate(items):
        rows, kc, vc = span(c, hd)
        q = qin_ref[sq, rows, kc]
        state_t = state_ref[sq, hd]
        o = _dot(a_ref[i], v_ref[sq, rows, vc]) + _dot_nt(q, state_t.astype(BF16))
        state_ref[sq, hd] = state_t * dec_ref[sq, c:c + 1, kc] + kv_ref[i]
        o = o * lax.rsqrt(jnp.mean(o * o, axis=-1, keepdims=True) + EPS)
        o_ref[sq, rows, vc] = (o * gate_ref[sq, rows, vc]).astype(BF16)


def _gla_core(qin, kin, kdec, v, gate, dec):
    b, s, dk = qin.shape
    dv = v.shape[2]
    tm = TOKEN_TILE
    nb = GLA_BATCH
    n_items = nb * GLA_HEADS * (tm // CHUNK)
    tok = lambda width: pl.BlockSpec((nb, tm, width), lambda bi, si: (bi, si, 0))
    return pl.pallas_call(
        _gla_core_kernel,
        grid=(b // nb, s // tm),
        in_specs=[tok(dk), tok(dk), tok(dk), tok(dv), tok(dv),
                  pl.BlockSpec((nb, tm // CHUNK, dk), lambda bi, si: (bi, si, 0))],
        out_specs=tok(dv),
        out_shape=jax.ShapeDtypeStruct((b, s, dv), BF16),
        scratch_shapes=[pltpu.VMEM((nb, GLA_HEADS, dv // GLA_HEADS, dk // GLA_HEADS), F32),
                        pltpu.VMEM((n_items, dv // GLA_HEADS, dk // GLA_HEADS), F32),
                        pltpu.VMEM((n_items, CHUNK, CHUNK), BF16)],
        compiler_params=_params(2),
        name="gla_core",
    )(qin, kin, kdec, v, gate, dec)


def _mix_ffn_kernel(x_ref, o_ref, mod_ref, g_ref, wo_ref, win_ref, wout_ref, post_g_ref, post_mod_ref,
                    out_ref, *rest, hidden, final):
    acc_ref = rest[-1]
    tm = x_ref.shape[0]
    n_chunks = hidden // FFN_CHUNK
    parts = [slice(r * tm // ROW_PARTS, (r + 1) * tm // ROW_PARTS) for r in range(ROW_PARTS)]

    for rows in parts:
        out_ref[rows, :] = x_ref[rows, :] + mod_ref[2:3, :] * _dot(o_ref[rows, :], wo_ref[...])
    h = [_rms_mod(out_ref[rows, :], g_ref[...], mod_ref[3:4, :], mod_ref[4:5, :]).astype(BF16)
         for rows in parts]

    def gate_up(t):
        r, j = divmod(t, n_chunks)
        cols = slice(j * FFN_CHUNK, (j + 1) * FFN_CHUNK)
        up_cols = slice(hidden + j * FFN_CHUNK, hidden + (j + 1) * FFN_CHUNK)
        return (_silu(_dot(h[r], win_ref[:, cols])) * _dot(h[r], win_ref[:, up_cols])).astype(BF16)

    def epilogue(rows):
        y = out_ref[rows, :] + mod_ref[5:6, :] * acc_ref[rows, :]
        if final:
            y = y * lax.rsqrt(jnp.mean(y * y, axis=-1, keepdims=True) + EPS) * post_g_ref[...]
        else:
            rest[0][rows, :] = _rms_mod(
                y, post_g_ref[...], post_mod_ref[0:1, :], post_mod_ref[1:2, :]).astype(BF16)
        out_ref[rows, :] = y

    act = gate_up(0)
    for t in range(ROW_PARTS * n_chunks):
        nxt = gate_up(t + 1) if t + 1 < ROW_PARTS * n_chunks else None
        r, j = divmod(t, n_chunks)
        down = _dot(act, wout_ref[j * FFN_CHUNK:(j + 1) * FFN_CHUNK, :])
        if j == 0:
            acc_ref[parts[r], :] = down
        else:
            acc_ref[parts[r], :] += down
        if j == n_chunks - 1:
            epilogue(parts[r])
        act = nxt


def _mix_ffn(x, o, mod, g, wo_stack, j, win_stack, wout_stack, post_g, layer, final):
    b, s, d = x.shape
    tm = TOKEN_TILE
    tok = pl.BlockSpec((None, tm, d), lambda bi, si: (bi, si, 0))
    nxt = layer if final else layer + 1
    post_g_spec = (_resident(post_g.shape) if final
                   else pl.BlockSpec((None, 1, d), lambda bi, si: (nxt, 0, 0)))
    outs = [jax.ShapeDtypeStruct((b, s, d), F32)] + ([] if final else [jax.ShapeDtypeStruct((b, s, d), BF16)])
    res = pl.pallas_call(
        functools.partial(_mix_ffn_kernel, hidden=wout_stack.shape[1], final=final),
        grid=(b, s // tm),
        in_specs=[
            tok, tok,
            pl.BlockSpec((None, None, N_MOD, d), lambda bi, si: (layer, bi, 0, 0)),
            pl.BlockSpec((None, 1, d), lambda bi, si: (layer, 0, 0)),
            _layer_of(wo_stack, j), _layer_of(win_stack, layer), _layer_of(wout_stack, layer),
            post_g_spec,
            pl.BlockSpec((None, None, N_MOD, d), lambda bi, si: (nxt, bi, 0, 0)),
        ],
        out_specs=[tok] * len(outs),
        out_shape=outs,
        scratch_shapes=[pltpu.VMEM((tm, d), F32)],
        compiler_params=_params(2),
        name="mix_ffn",
    )(x, o, mod, g, wo_stack, win_stack, wout_stack, post_g, mod)
    return (res[0], None) if final else (res[0], res[1])


def kernel(x, c, ada_w, ada_b, norm1_g, norm2_g, ffn_w_in, ffn_w_out, fox_w_in, fox_b_f, fox_w_out,
           gla_w_in, gla_w_a2, gla_b_a, gla_g_o, gla_w_out, final_g):
    depth, d, _ = ada_w.shape
    b = x.shape[0]
    assert x.shape[1] % TOKEN_TILE == 0 and TOKEN_TILE % ATTN_TILE == 0
    assert ffn_w_out.shape[1] % FFN_CHUNK == 0
    mod = _adaln_mod(c, ada_w, ada_b).reshape(depth, b, N_MOD, d)
    g1 = norm1_g.reshape(depth, 1, d)
    g2 = norm2_g.reshape(depth, 1, d)
    fg = final_g.reshape(1, d)
    ffn_in, ffn_out = ffn_w_in.astype(BF16), ffn_w_out.astype(BF16)
    fox_in, fox_out = fox_w_in.astype(BF16), fox_w_out.astype(BF16)
    gla_in, gla_out = gla_w_in.astype(BF16), gla_w_out.astype(BF16)
    rank = gla_w_a2.shape[1]
    h = None
    for i in range(depth):
        j = i // 2
        src, normed = (x, False) if h is None else (h, True)
        if i % 2 == 0:
            w_f = fox_in[j][:, 3 * d:]
            pad = jnp.zeros((d, LANES - N_SPLIT * FOX_HEADS), BF16)
            w = jnp.concatenate([fox_in[j][:, :3 * d]] + [w_f] * N_SPLIT + [pad], axis=1)
            b_f = jnp.concatenate(
                [fox_b_f[j]] * N_SPLIT + [jnp.zeros((LANES - N_SPLIT * FOX_HEADS,), F32)])
            qt, k, vt, qxt, kx = _fox_proj(src, mod, g1, w, b_f.reshape(1, LANES), i, normed)
            o = _fox_attn(qt, qxt, k, kx, vt)
            w_o = fox_out
        else:
            n_main = gla_in.shape[2] - rank
            wa = jnp.pad(gla_in[j][:, n_main:], ((0, 0), (0, LANES - rank)))
            wa2 = jnp.pad(gla_w_a2[j], ((0, LANES - rank), (0, 0))).astype(BF16)
            qin, kin, kdec, v, gate, dec = _gla_proj(
                src, mod, g1, gla_in, j, wa, wa2, gla_b_a[j].reshape(1, -1),
                gla_g_o[j].reshape(1, -1), i, normed)
            o = _gla_core(qin, kin, kdec, v, gate, dec)
            w_o = gla_out
        final = i == depth - 1
        x, h = _mix_ffn(x, o, mod, g2, w_o, j, ffn_in, ffn_out, fg if final else g1, i, final)
    return x
```

```python
import functools
import math

import jax
import jax.numpy as jnp
from jax import lax
from jax.experimental import pallas as pl
from jax.experimental.pallas import tpu as pltpu

F32 = jnp.float32
BF16 = jnp.bfloat16

EPS = 1e-6
LOG2_E = math.log2(math.e)
N_MOD = 6
CHUNK = 64
FOX_HEADS = 16
GLA_HEADS = 4
GLA_TAU = 16.0
LANES = 128
BIAS_LANES = 8
N_SPLIT = 3
GLA_SPLIT = 2
SUM_ROWS = 16

TOKEN_TILE = 512
ATTN_TILE = 256
ATTN_PAIRS = 2
PIPE_DEPTH = 6
GLA_BATCH = 2
SLABS_AHEAD = 3
FFN_CHUNK = 256
ROW_PARTS = 2
CUM_BLOCK = 128
VMEM_LIMIT = 56 * 1024 * 1024


def _params(n_grid, flags=None):
    return pltpu.CompilerParams(
        dimension_semantics=("arbitrary",) * n_grid, vmem_limit_bytes=VMEM_LIMIT, flags=flags)


def _resident(shape):
    return pl.BlockSpec(shape, lambda *_: (0,) * len(shape), pipeline_mode=pl.Buffered(1))


def _layer_of(stack, idx):
    return pl.BlockSpec((None,) + stack.shape[1:], lambda *_: (idx, 0, 0),
                        pipeline_mode=pl.Buffered(1))


def _rms_mod(x, g, shift, scale):
    y = x * lax.rsqrt(jnp.mean(x * x, axis=-1, keepdims=True) + EPS) * g
    return y * (1.0 + scale) + shift


def _input_h(x_ref, mod_ref, g_ref, normed):
    if normed:
        return x_ref[...]
    return _rms_mod(x_ref[...], g_ref[...], mod_ref[0:1, :], mod_ref[1:2, :]).astype(BF16)


def _log_sigmoid(x):
    return jnp.minimum(x, 0.0) - jnp.log(1.0 + jnp.exp(-jnp.abs(x)))


def _silu(x):
    return x * jax.nn.sigmoid(x)


def _split_bf16(x, n_terms=N_SPLIT):
    terms = []
    for _ in range(n_terms - 1):
        t = x.astype(BF16)
        terms.append(t)
        x = x - t.astype(F32)
    terms.append(x.astype(BF16))
    return terms


def _dot(a, b):
    return jnp.dot(a, b, preferred_element_type=F32)


def _dot_nt(a, b):
    return lax.dot_general(a, b, (((1,), (1,)), ((), ())), preferred_element_type=F32)


def _dot_tn(a, b):
    return lax.dot_general(a, b, (((0,), (0,)), ((), ())), preferred_element_type=F32)


def _adaln_kernel(c_ref, w_ref, b_ref, o_ref):
    ca = _silu(c_ref[...]).astype(BF16)
    o_ref[...] = _dot(ca, w_ref[...].astype(BF16)) + b_ref[...]


def _adaln_mod(c, ada_w, ada_b):
    depth, d, n = ada_w.shape
    b = c.shape[0]
    bn = 1024
    return pl.pallas_call(
        _adaln_kernel,
        grid=(depth, n // bn),
        in_specs=[
            pl.BlockSpec((b, d), lambda i, j: (0, 0)),
            pl.BlockSpec((None, d, bn), lambda i, j: (i, 0, j)),
            pl.BlockSpec((None, 1, bn), lambda i, j: (i, 0, j)),
        ],
        out_specs=pl.BlockSpec((None, b, bn), lambda i, j: (i, 0, j)),
        out_shape=jax.ShapeDtypeStruct((depth, b, n), F32),
        compiler_params=_params(2),
        name="adaln_mod",
    )(c, ada_w, ada_b.reshape(depth, 1, n))


def _fox_proj_kernel(x_ref, mod_ref, g_ref, w_ref, bf_ref,
                     qt_ref, k_ref, vt_ref, qxt_ref, kx_ref, carry_ref, *, d, scale, normed):
    tm = x_ref.shape[0]
    part = tm // ROW_PARTS
    parts = [slice(r * part, (r + 1) * part) for r in range(ROW_PARTS)]

    @pl.when(pl.program_id(1) == 0)
    def _():
        carry_ref[...] = jnp.zeros_like(carry_ref)

    lane = lax.broadcasted_iota(jnp.int32, (1, LANES), 1)
    used = lane < N_SPLIT * FOX_HEADS
    group = lane // FOX_HEADS
    row = lax.broadcasted_iota(jnp.int32, (part, part), 0)
    col = lax.broadcasted_iota(jnp.int32, (part, part), 1)
    tril = jnp.where(row >= col, 1.0, 0.0).astype(BF16)
    src = lax.broadcasted_iota(jnp.int32, (LANES, LANES), 0)
    dst = lax.broadcasted_iota(jnp.int32, (LANES, LANES), 1)
    same_head = (src % FOX_HEADS == dst // BIAS_LANES) & (src < N_SPLIT * FOX_HEADS)
    place_q = jnp.where(same_head & (src // FOX_HEADS == dst % BIAS_LANES), 1.0, 0.0).astype(BF16)
    place_k = jnp.where(same_head & (src // FOX_HEADS + N_SPLIT == dst % BIAS_LANES), 1.0, 0.0).astype(BF16)
    pos = lane % BIAS_LANES
    ones_q = jnp.where((pos >= N_SPLIT) & (pos < 2 * N_SPLIT), 1.0, 0.0)
    ones_k = jnp.where(pos < N_SPLIT, 1.0, 0.0)

    h = [_input_h(x_ref.at[rows, :], mod_ref, g_ref, normed) for rows in parts]
    proj = [_dot(hp, w_ref[...]) for hp in h]
    log_f = []
    for rows, pr in zip(parts, proj):
        qt_ref[:, rows] = (pr[:, :d] * (scale * LOG2_E)).T.astype(BF16)
        k_ref[rows, :] = pr[:, d:2 * d].astype(BF16)
        vt_ref[:, rows] = pr[:, 2 * d:3 * d].T.astype(BF16)
        log_f.append(jnp.where(used, _log_sigmoid(pr[:, 3 * d:] + bf_ref[...]), 0.0))
    sums = [_dot(tril, jnp.concatenate(_split_bf16(lf), axis=1)) for lf in log_f]
    carry = carry_ref[...]
    for rows, sm in zip(parts, sums):
        cum = carry + sum(sm[:, t * LANES:(t + 1) * LANES] for t in range(N_SPLIT))
        carry = cum[part - 1:part, :]
        terms = _split_bf16(cum * LOG2_E)
        packed = jnp.zeros_like(terms[0])
        for t in range(N_SPLIT):
            packed = jnp.where(group == t, terms[t], packed)
        qxt_ref[:, rows] = (_dot(packed, place_q) + ones_q).T.astype(BF16)
        kx_ref[rows, :] = (ones_k - _dot(packed, place_k)).astype(BF16)
    carry_ref[...] = carry


def _fox_proj(x, mod, g, w, b_f, layer, normed):
    b, s, d = x.shape
    tm = TOKEN_TILE
    tok = lambda width: pl.BlockSpec((None, tm, width), lambda bi, si: (bi, si, 0))
    tr = lambda width: pl.BlockSpec((None, width, tm), lambda bi, si: (bi, 0, si))
    out_bf16 = lambda width: jax.ShapeDtypeStruct((b, s, width), BF16)
    out_t = lambda width: jax.ShapeDtypeStruct((b, width, s), BF16)
    return pl.pallas_call(
        functools.partial(_fox_proj_kernel, d=d, scale=(d // FOX_HEADS) ** -0.5, normed=normed),
        grid=(b, s // tm),
        in_specs=[
            tok(d),
            pl.BlockSpec((None, None, N_MOD, d), lambda bi, si: (layer, bi, 0, 0)),
            pl.BlockSpec((None, 1, d), lambda bi, si: (layer, 0, 0)),
            _resident(w.shape),
            _resident(b_f.shape),
        ],
        out_specs=[tr(d), tok(d), tr(d), tr(LANES), tok(LANES)],
        out_shape=[out_t(d), out_bf16(d), out_t(d), out_t(LANES), out_bf16(LANES)],
        scratch_shapes=[pltpu.VMEM((1, LANES), F32)],
        compiler_params=_params(2),
        name="fox_proj",
    )(x, mod, g, w, b_f)


def _fox_attn_kernel(qt_ref, qxt_ref, k_ref, kx_ref, vt_ref, o_ref, s_ref, cmax_ref, p_ref):
    tq = ATTN_TILE
    hd = LANES // 2
    n_blocks = k_ref.shape[0] // tq
    n_slots = s_ref.shape[0]
    chan = lax.broadcasted_iota(jnp.int32, (2 * LANES, 1), 0)
    key = lax.broadcasted_iota(jnp.int32, (tq, tq), 0)
    qry = lax.broadcasted_iota(jnp.int32, (tq, tq), 1)
    causal = key <= qry
    ones = jnp.ones((SUM_ROWS, tq), BF16)
    keep = {}
    for pr in range(ATTN_PAIRS):
        for hh in range(2):
            head = 2 * (pl.program_id(1) * ATTN_PAIRS + pr) + hh
            bias_lo = LANES + BIAS_LANES * head
            keep[(pr, hh)] = (((chan >= hh * hd) & (chan < (hh + 1) * hd))
                              | ((chan >= bias_lo) & (chan < bias_lo + BIAS_LANES)))

    items = [(pr, n, hh, j) for pr in range(ATTN_PAIRS) for n in range(n_blocks)
             for j in [n] + list(range(n)) for hh in range(2)]

    def blk(i):
        return slice(i * tq, (i + 1) * tq)

    def cols(pr):
        return slice(pr * LANES, (pr + 1) * LANES)

    head_q = {}

    def scores(t):
        pr, n, hh, j = items[t]
        if (pr, n, hh) not in head_q:
            q_full = jnp.concatenate([qt_ref[cols(pr), blk(n)], qxt_ref[:, blk(n)]], axis=0)
            head_q[(pr, n, hh)] = jnp.where(keep[(pr, hh)], q_full, jnp.zeros_like(q_full))
        kb = jnp.concatenate([k_ref[blk(j), cols(pr)], kx_ref[blk(j), :]], axis=1)
        s = _dot(kb, head_q[(pr, n, hh)])
        if j == n:
            s = jnp.where(causal, s, -jnp.inf)
        s_ref[t % n_slots] = s
        cmax_ref[t % n_slots] = jnp.max(s, axis=0, keepdims=True)

    run_max = {}
    rescale = {}

    def weights(t):
        grp = items[t][:3]
        m_old = run_max.get(grp)
        m_new = cmax_ref[t % n_slots] if m_old is None else jnp.maximum(m_old, cmax_ref[t % n_slots])
        p_ref[t % n_slots] = jnp.exp2(s_ref[t % n_slots] - m_new).astype(BF16)
        rescale[t] = None if m_old is None else jnp.exp2(m_old - m_new)
        run_max[grp] = m_new

    assert n_slots > PIPE_DEPTH + 1
    for t in range(PIPE_DEPTH):
        scores(t)
    weights(0)
    acc = {}
    for t, (pr, n, hh, j) in enumerate(items):
        if t + PIPE_DEPTH < len(items):
            scores(t + PIPE_DEPTH)
        if t + 1 < len(items):
            weights(t + 1)
        rows = slice(pr * LANES + hh * hd, pr * LANES + (hh + 1) * hd)
        vt = jnp.concatenate([vt_ref[rows, blk(j)], ones], axis=0)
        pv = _dot(vt, p_ref[t % n_slots])
        acc[(pr, n, hh)] = pv if rescale[t] is None else rescale[t] * acc[(pr, n, hh)] + pv
        if j == (n - 1 if n else 0) and hh == 1:
            o_t = [acc[(pr, n, h)][:hd, :] / acc[(pr, n, h)][hd:hd + 1, :] for h in range(2)]
            o_ref[blk(n), cols(pr)] = jnp.concatenate(o_t, axis=0).T.astype(BF16)


def _fox_attn(qt, qxt, k, kx, vt):
    b, s, d = k.shape
    tq = ATTN_TILE
    n_slots = (s // tq) * (s // tq + 1)
    width = ATTN_PAIRS * LANES
    seq_blk = lambda w, col: pl.BlockSpec((None, s, w), col)
    chan_blk = lambda w, col: pl.BlockSpec((None, w, s), col)
    return pl.pallas_call(
        _fox_attn_kernel,
        grid=(b, d // width),
        in_specs=[
            chan_blk(width, lambda bi, p: (bi, p, 0)),
            chan_blk(LANES, lambda bi, p: (bi, 0, 0)),
            seq_blk(width, lambda bi, p: (bi, 0, p)),
            seq_blk(LANES, lambda bi, p: (bi, 0, 0)),
            chan_blk(width, lambda bi, p: (bi, p, 0)),
        ],
        out_specs=seq_blk(width, lambda bi, p: (bi, 0, p)),
        out_shape=jax.ShapeDtypeStruct((b, s, d), BF16),
        scratch_shapes=[pltpu.VMEM((n_slots, tq, tq), F32), pltpu.VMEM((n_slots, 1, tq), F32),
                        pltpu.VMEM((n_slots, tq, tq), BF16)],
        compiler_params=_params(2),
        name="fox_attn",
    )(qt, qxt, k, kx, vt)


def _gla_proj_kernel(x_ref, mod_ref, g_ref, w_ref, wa_ref, wa2_ref, ba_ref, go_ref,
                     qin_ref, kin_ref, kdec_ref, v_ref, gate_ref, dec_ref, b_ref,
                     *, dk, dv, scale, normed):
    tm = x_ref.shape[0]
    h = _input_h(x_ref, mod_ref, g_ref, normed)
    n_chunks = tm // CHUNK
    slab = 2 * dv // n_chunks

    def vr_slab(c):
        vr = _dot(h, w_ref[:, 2 * dk + c * slab:2 * dk + (c + 1) * slab])
        if (c + 1) * slab <= dv:
            v_ref[:, c * slab:(c + 1) * slab] = vr.astype(BF16)
        else:
            cols = slice(c * slab - dv, (c + 1) * slab - dv)
            gate_ref[:, cols] = _silu(vr) * go_ref[:, cols]

    a_lr = _dot(h, wa_ref[...]).astype(BF16)
    qk = _dot(h, w_ref[:, :2 * dk])
    log_alpha = _log_sigmoid(_dot(a_lr, wa2_ref[...]) + ba_ref[...]) / GLA_TAU
    for c in range(SLABS_AHEAD):
        vr_slab(c)

    row = lax.broadcasted_iota(jnp.int32, (CUM_BLOCK, CUM_BLOCK), 0)
    col = lax.broadcasted_iota(jnp.int32, (CUM_BLOCK, CUM_BLOCK), 1)
    tril = jnp.where((row >= col) & (row // CHUNK == col // CHUNK), 1.0, 0.0).astype(BF16)
    terms = jnp.concatenate(_split_bf16(log_alpha, GLA_SPLIT), axis=1)
    for i in range(tm // CUM_BLOCK):
        rows = slice(i * CUM_BLOCK, (i + 1) * CUM_BLOCK)
        sums = _dot(tril, terms[rows, :])
        b_ref[rows, :] = sum(sums[:, t * dk:(t + 1) * dk] for t in range(GLA_SPLIT))

    for c in range(n_chunks):
        if c + SLABS_AHEAD < n_chunks:
            vr_slab(c + SLABS_AHEAD)
        rows = slice(c * CHUNK, (c + 1) * CHUNK)
        b = b_ref[rows, :]
        b_last = b[CHUNK - 1:, :]
        k = qk[rows, dk:2 * dk]
        qin_ref[rows, :] = (qk[rows, :dk] * scale * jnp.exp(b)).astype(BF16)
        kin_ref[rows, :] = (k * jnp.exp(-b)).astype(BF16)
        kdec_ref[rows, :] = (k * jnp.exp(b_last - b)).astype(BF16)
        dec_ref[c:c + 1, :] = jnp.exp(b_last)


def _gla_proj(x, mod, g, w_stack, j, wa, wa2, ba, g_o, layer, normed):
    b, s, d = x.shape
    tm = TOKEN_TILE
    dk = wa2.shape[1]
    dv = g_o.shape[1]
    tok = lambda width: pl.BlockSpec((None, tm, width), lambda bi, si: (bi, si, 0))
    out = lambda width, dt: jax.ShapeDtypeStruct((b, s, width), dt)
    return pl.pallas_call(
        functools.partial(_gla_proj_kernel, dk=dk, dv=dv, scale=(dk // GLA_HEADS) ** -0.5,
                          normed=normed),
        grid=(b, s // tm),
        in_specs=[
            tok(d),
            pl.BlockSpec((None, None, N_MOD, d), lambda bi, si: (layer, bi, 0, 0)),
            pl.BlockSpec((None, 1, d), lambda bi, si: (layer, 0, 0)),
            _layer_of(w_stack, j),
            _resident(wa.shape),
            _resident(wa2.shape),
            _resident(ba.shape),
            _resident(g_o.shape),
        ],
        out_specs=[tok(dk), tok(dk), tok(dk), tok(dv), tok(dv),
                   pl.BlockSpec((None, tm // CHUNK, dk), lambda bi, si: (bi, si, 0))],
        out_shape=[out(dk, BF16), out(dk, BF16), out(dk, BF16), out(dv, BF16), out(dv, F32),
                   jax.ShapeDtypeStruct((b, s // CHUNK, dk), F32)],
        scratch_shapes=[pltpu.VMEM((tm, dk), F32)],
        compiler_params=_params(2),
        name="gla_proj",
    )(x, mod, g, w_stack, wa, wa2, ba, g_o)


def _gla_core_kernel(qin_ref, kin_ref, kdec_ref, v_ref, gate_ref, dec_ref,
                     o_ref, state_ref, kv_ref, a_ref):
    n_seq, tm, dk = qin_ref.shape
    hk = dk // GLA_HEADS
    hv = v_ref.shape[2] // GLA_HEADS

    @pl.when(pl.program_id(1) == 0)
    def _():
        state_ref[...] = jnp.zeros_like(state_ref)

    row = lax.broadcasted_iota(jnp.int32, (CHUNK, CHUNK), 0)
    col = lax.broadcasted_iota(jnp.int32, (CHUNK, CHUNK), 1)
    causal = row >= col

    items = [(c, hd, sq) for c in range(tm // CHUNK) for hd in range(GLA_HEADS)
             for sq in range(n_seq)]

    def span(c, hd):
        return (slice(c * CHUNK, (c + 1) * CHUNK), slice(hd * hk, (hd + 1) * hk),
                slice(hd * hv, (hd + 1) * hv))

    for i, (c, hd, sq) in enumerate(items):
        rows, kc, vc = span(c, hd)
        kv_ref[i] = _dot_tn(v_ref[sq, rows, vc], kdec_ref[sq, rows, kc])
        a = _dot_nt(qin_ref[sq, rows, kc], kin_ref[sq, rows, kc])
        a_ref[i] = jnp.where(causal, a, 0.0).astype(BF16)

    for i, (c, hd, sq) in enumerate(items):
        rows, kc, vc = span(c, hd)
        q = qin_ref[sq, rows, kc]
        state_t = state_ref[sq, hd]
        o = _dot(a_ref[i], v_ref[sq, rows, vc]) + _dot_nt(q, state_t.astype(BF16))
        state_ref[sq, hd] = state_t * dec_ref[sq, c:c + 1, kc] + kv_ref[i]
        o = o * lax.rsqrt(jnp.mean(o * o, axis=-1, keepdims=True) + EPS)
        o_ref[sq, rows, vc] = (o * gate_ref[sq, rows, vc]).astype(BF16)


def _gla_core(qin, kin, kdec, v, gate, dec):
    b, s, dk = qin.shape
    dv = v.shape[2]
    tm = TOKEN_TILE
    nb = GLA_BATCH
    n_items = nb * GLA_HEADS * (tm // CHUNK)
    tok = lambda width: pl.BlockSpec((nb, tm, width), lambda bi, si: (bi, si, 0))
    return pl.pallas_call(
        _gla_core_kernel,
        grid=(b // nb, s // tm),
        in_specs=[tok(dk), tok(dk), tok(dk), tok(dv), tok(dv),
                  pl.BlockSpec((nb, tm // CHUNK, dk), lambda bi, si: (bi, si, 0))],
        out_specs=tok(dv),
        out_shape=jax.ShapeDtypeStruct((b, s, dv), BF16),
        scratch_shapes=[pltpu.VMEM((nb, GLA_HEADS, dv // GLA_HEADS, dk // GLA_HEADS), F32),
                        pltpu.VMEM((n_items, dv // GLA_HEADS, dk // GLA_HEADS), F32),
                        pltpu.VMEM((n_items, CHUNK, CHUNK), BF16)],
        compiler_params=_params(2),
        name="gla_core",
    )(qin, kin, kdec, v, gate, dec)


def _mix_ffn_kernel(x_ref, o_ref, mod_ref, g_ref, wo_ref, win_ref, wout_ref, post_g_ref, post_mod_ref,
                    out_ref, *rest, hidden, final):
    acc_ref = rest[-1]
    tm = x_ref.shape[0]
    n_chunks = hidden // FFN_CHUNK
    parts = [slice(r * tm // ROW_PARTS, (r + 1) * tm // ROW_PARTS) for r in range(ROW_PARTS)]

    for rows in parts:
        out_ref[rows, :] = x_ref[rows, :] + mod_ref[2:3, :] * _dot(o_ref[rows, :], wo_ref[...])
    h = [_rms_mod(out_ref[rows, :], g_ref[...], mod_ref[3:4, :], mod_ref[4:5, :]).astype(BF16)
         for rows in parts]

    def gate_up(t):
        r, j = divmod(t, n_chunks)
        cols = slice(j * FFN_CHUNK, (j + 1) * FFN_CHUNK)
        up_cols = slice(hidden + j * FFN_CHUNK, hidden + (j + 1) * FFN_CHUNK)
        return (_silu(_dot(h[r], win_ref[:, cols])) * _dot(h[r], win_ref[:, up_cols])).astype(BF16)

    def epilogue(rows):
        y = out_ref[rows, :] + mod_ref[5:6, :] * acc_ref[rows, :]
        if final:
            y = y * lax.rsqrt(jnp.mean(y * y, axis=-1, keepdims=True) + EPS) * post_g_ref[...]
        else:
            rest[0][rows, :] = _rms_mod(
                y, post_g_ref[...], post_mod_ref[0:1, :], post_mod_ref[1:2, :]).astype(BF16)
        out_ref[rows, :] = y

    act = gate_up(0)
    for t in range(ROW_PARTS * n_chunks):
        nxt = gate_up(t + 1) if t + 1 < ROW_PARTS * n_chunks else None
        r, j = divmod(t, n_chunks)
        down = _dot(act, wout_ref[j * FFN_CHUNK:(j + 1) * FFN_CHUNK, :])
        if j == 0:
            acc_ref[parts[r], :] = down
        else:
            acc_ref[parts[r], :] += down
        if j == n_chunks - 1:
            epilogue(parts[r])
        act = nxt


def _mix_ffn(x, o, mod, g, wo_stack, j, win_stack, wout_stack, post_g, layer, final):
    b, s, d = x.shape
    tm = TOKEN_TILE
    tok = pl.BlockSpec((None, tm, d), lambda bi, si: (bi, si, 0))
    nxt = layer if final else layer + 1
    post_g_spec = (_resident(post_g.shape) if final
                   else pl.BlockSpec((None, 1, d), lambda bi, si: (nxt, 0, 0)))
    outs = [jax.ShapeDtypeStruct((b, s, d), F32)] + ([] if final else [jax.ShapeDtypeStruct((b, s, d), BF16)])
    res = pl.pallas_call(
        functools.partial(_mix_ffn_kernel, hidden=wout_stack.shape[1], final=final),
        grid=(b, s // tm),
        in_specs=[
            tok, tok,
            pl.BlockSpec((None, None, N_MOD, d), lambda bi, si: (layer, bi, 0, 0)),
            pl.BlockSpec((None, 1, d), lambda bi, si: (layer, 0, 0)),
            _layer_of(wo_stack, j), _layer_of(win_stack, layer), _layer_of(wout_stack, layer),
            post_g_spec,
            pl.BlockSpec((None, None, N_MOD, d), lambda bi, si: (nxt, bi, 0, 0)),
        ],
        out_specs=[tok] * len(outs),
        out_shape=outs,
        scratch_shapes=[pltpu.VMEM((tm, d), F32)],
        compiler_params=_params(2),
        name="mix_ffn",
    )(x, o, mod, g, wo_stack, win_stack, wout_stack, post_g, mod)
    return (res[0], None) if final else (res[0], res[1])


def kernel(x, c, ada_w, ada_b, norm1_g, norm2_g, ffn_w_in, ffn_w_out, fox_w_in, fox_b_f, fox_w_out,
           gla_w_in, gla_w_a2, gla_b_a, gla_g_o, gla_w_out, final_g):
    depth, d, _ = ada_w.shape
    b = x.shape[0]
    assert x.shape[1] % TOKEN_TILE == 0 and TOKEN_TILE % ATTN_TILE == 0
    assert ffn_w_out.shape[1] % FFN_CHUNK == 0
    mod = _adaln_mod(c, ada_w, ada_b).reshape(depth, b, N_MOD, d)
    g1 = norm1_g.reshape(depth, 1, d)
    g2 = norm2_g.reshape(depth, 1, d)
    fg = final_g.reshape(1, d)
    ffn_in, ffn_out = ffn_w_in.astype(BF16), ffn_w_out.astype(BF16)
    fox_in, fox_out = fox_w_in.astype(BF16), fox_w_out.astype(BF16)
    gla_in, gla_out = gla_w_in.astype(BF16), gla_w_out.astype(BF16)
    rank = gla_w_a2.shape[1]
    h = None
    for i in range(depth):
        j = i // 2
        src, normed = (x, False) if h is None else (h, True)
        if i % 2 == 0:
            w_f = fox_in[j][:, 3 * d:]
            pad = jnp.zeros((d, LANES - N_SPLIT * FOX_HEADS), BF16)
            w = jnp.concatenate([fox_in[j][:, :3 * d]] + [w_f] * N_SPLIT + [pad], axis=1)
            b_f = jnp.concatenate(
                [fox_b_f[j]] * N_SPLIT + [jnp.zeros((LANES - N_SPLIT * FOX_HEADS,), F32)])
            qt, k, vt, qxt, kx = _fox_proj(src, mod, g1, w, b_f.reshape(1, LANES), i, normed)
            o = _fox_attn(qt, qxt, k, kx, vt)
            w_o = fox_out
        else:
            n_main = gla_in.shape[2] - rank
            wa = jnp.pad(gla_in[j][:, n_main:], ((0, 0), (0, LANES - rank)))
            wa2 = jnp.pad(gla_w_a2[j], ((0, LANES - rank), (0, 0))).astype(BF16)
            qin, kin, kdec, v, gate, dec = _gla_proj(
                src, mod, g1, gla_in, j, wa, wa2, gla_b_a[j].reshape(1, -1),
                gla_g_o[j].reshape(1, -1), i, normed)
            o = _gla_core(qin, kin, kdec, v, gate, dec)
            w_o = gla_out
        final = i == depth - 1
        x, h = _mix_ffn(x, o, mod, g2, w_o, j, ffn_in, ffn_out, fg if final else g1, i, final)
    return x
```

```python
import functools
import math

import jax
import jax.numpy as jnp
from jax import lax
from jax.experimental import pallas as pl
from jax.experimental.pallas import tpu as pltpu

F32 = jnp.float32
BF16 = jnp.bfloat16

EPS = 1e-6
LOG2_E = math.log2(math.e)
N_MOD = 6
CHUNK = 64
FOX_HEADS = 16
GLA_HEADS = 4
GLA_TAU = 16.0
LANES = 128
BIAS_LANES = 8
N_SPLIT = 3
GLA_SPLIT = 2
SUM_ROWS = 16

TOKEN_TILE = 512
ATTN_TILE = 256
ATTN_PAIRS = 2
PIPE_DEPTH = 6
GLA_BATCH = 2
SLABS_AHEAD = 3
FFN_CHUNK = 256
ROW_PARTS = 2
FOX_TILE = 1024
PART_ROWS = 256
CUM_BLOCK = 128
VMEM_LIMIT = 56 * 1024 * 1024


def _params(n_grid, flags=None):
    return pltpu.CompilerParams(
        dimension_semantics=("arbitrary",) * n_grid, vmem_limit_bytes=VMEM_LIMIT, flags=flags)


def _resident(shape):
    return pl.BlockSpec(shape, lambda *_: (0,) * len(shape), pipeline_mode=pl.Buffered(1))


def _layer_of(stack, idx):
    return pl.BlockSpec((None,) + stack.shape[1:], lambda *_: (idx, 0, 0),
                        pipeline_mode=pl.Buffered(1))


def _rms_mod(x, g, shift, scale):
    y = x * lax.rsqrt(jnp.mean(x * x, axis=-1, keepdims=True) + EPS) * g
    return y * (1.0 + scale) + shift


def _input_h(x_ref, mod_ref, g_ref, normed):
    if normed:
        return x_ref[...]
    return _rms_mod(x_ref[...], g_ref[...], mod_ref[0:1, :], mod_ref[1:2, :]).astype(BF16)


def _log_sigmoid(x):
    return jnp.minimum(x, 0.0) - jnp.log(1.0 + jnp.exp(-jnp.abs(x)))


def _silu(x):
    return x * jax.nn.sigmoid(x)


def _split_bf16(x, n_terms=N_SPLIT):
    terms = []
    for _ in range(n_terms - 1):
        t = x.astype(BF16)
        terms.append(t)
        x = x - t.astype(F32)
    terms.append(x.astype(BF16))
    return terms


def _dot(a, b):
    return jnp.dot(a, b, preferred_element_type=F32)


def _dot_nt(a, b):
    return lax.dot_general(a, b, (((1,), (1,)), ((), ())), preferred_element_type=F32)


def _dot_tn(a, b):
    return lax.dot_general(a, b, (((0,), (0,)), ((), ())), preferred_element_type=F32)


def _adaln_kernel(c_ref, w_ref, b_ref, o_ref):
    ca = _silu(c_ref[...]).astype(BF16)
    o_ref[...] = _dot(ca, w_ref[...].astype(BF16)) + b_ref[...]


def _adaln_mod(c, ada_w, ada_b):
    depth, d, n = ada_w.shape
    b = c.shape[0]
    bn = 1024
    return pl.pallas_call(
        _adaln_kernel,
        grid=(depth, n // bn),
        in_specs=[
            pl.BlockSpec((b, d), lambda i, j: (0, 0)),
            pl.BlockSpec((None, d, bn), lambda i, j: (i, 0, j)),
            pl.BlockSpec((None, 1, bn), lambda i, j: (i, 0, j)),
        ],
        out_specs=pl.BlockSpec((None, b, bn), lambda i, j: (i, 0, j)),
        out_shape=jax.ShapeDtypeStruct((depth, b, n), F32),
        compiler_params=_params(2),
        name="adaln_mod",
    )(c, ada_w, ada_b.reshape(depth, 1, n))


def _fox_proj_kernel(x_ref, mod_ref, g_ref, w_ref, bf_ref,
                     qt_ref, k_ref, vt_ref, qxt_ref, kx_ref, carry_ref, *, d, scale, normed):
    tm = x_ref.shape[0]
    part = PART_ROWS
    parts = [slice(r * part, (r + 1) * part) for r in range(tm // part)]

    @pl.when(pl.program_id(1) == 0)
    def _():
        carry_ref[...] = jnp.zeros_like(carry_ref)

    lane = lax.broadcasted_iota(jnp.int32, (1, LANES), 1)
    used = lane < N_SPLIT * FOX_HEADS
    group = lane // FOX_HEADS
    row = lax.broadcasted_iota(jnp.int32, (part, part), 0)
    col = lax.broadcasted_iota(jnp.int32, (part, part), 1)
    tril = jnp.where(row >= col, 1.0, 0.0).astype(BF16)
    src = lax.broadcasted_iota(jnp.int32, (LANES, LANES), 0)
    dst = lax.broadcasted_iota(jnp.int32, (LANES, LANES), 1)
    same_head = (src % FOX_HEADS == dst // BIAS_LANES) & (src < N_SPLIT * FOX_HEADS)
    place_q = jnp.where(same_head & (src // FOX_HEADS == dst % BIAS_LANES), 1.0, 0.0).astype(BF16)
    place_k = jnp.where(same_head & (src // FOX_HEADS + N_SPLIT == dst % BIAS_LANES), 1.0, 0.0).astype(BF16)
    pos = lane % BIAS_LANES
    ones_q = jnp.where((pos >= N_SPLIT) & (pos < 2 * N_SPLIT), 1.0, 0.0)
    ones_k = jnp.where(pos < N_SPLIT, 1.0, 0.0)

    h = [_input_h(x_ref.at[rows, :], mod_ref, g_ref, normed) for rows in parts]
    proj = [_dot(hp, w_ref[...]) for hp in h]
    log_f = []
    for rows, pr in zip(parts, proj):
        qt_ref[:, rows] = (pr[:, :d] * (scale * LOG2_E)).T.astype(BF16)
        k_ref[rows, :] = pr[:, d:2 * d].astype(BF16)
        vt_ref[:, rows] = pr[:, 2 * d:3 * d].T.astype(BF16)
        log_f.append(jnp.where(used, _log_sigmoid(pr[:, 3 * d:] + bf_ref[...]), 0.0))
    sums = [_dot(tril, jnp.concatenate(_split_bf16(lf), axis=1)) for lf in log_f]
    carry = carry_ref[...]
    for rows, sm in zip(parts, sums):
        cum = carry + sum(sm[:, t * LANES:(t + 1) * LANES] for t in range(N_SPLIT))
        carry = cum[part - 1:part, :]
        terms = _split_bf16(cum * LOG2_E)
        packed = jnp.zeros_like(terms[0])
        for t in range(N_SPLIT):
            packed = jnp.where(group == t, terms[t], packed)
        qxt_ref[:, rows] = (_dot(packed, place_q) + ones_q).T.astype(BF16)
        kx_ref[rows, :] = (ones_k - _dot(packed, place_k)).astype(BF16)
    carry_ref[...] = carry


def _fox_proj(x, mod, g, w, b_f, layer, normed):
    b, s, d = x.shape
    tm = FOX_TILE
    tok = lambda width: pl.BlockSpec((None, tm, width), lambda bi, si: (bi, si, 0))
    tr = lambda width: pl.BlockSpec((None, width, tm), lambda bi, si: (bi, 0, si))
    out_bf16 = lambda width: jax.ShapeDtypeStruct((b, s, width), BF16)
    out_t = lambda width: jax.ShapeDtypeStruct((b, width, s), BF16)
    return pl.pallas_call(
        functools.partial(_fox_proj_kernel, d=d, scale=(d // FOX_HEADS) ** -0.5, normed=normed),
        grid=(b, s // tm),
        in_specs=[
            tok(d),
            pl.BlockSpec((None, None, N_MOD, d), lambda bi, si: (layer, bi, 0, 0)),
            pl.BlockSpec((None, 1, d), lambda bi, si: (layer, 0, 0)),
            _resident(w.shape),
            _resident(b_f.shape),
        ],
        out_specs=[tr(d), tok(d), tr(d), tr(LANES), tok(LANES)],
        out_shape=[out_t(d), out_bf16(d), out_t(d), out_t(LANES), out_bf16(LANES)],
        scratch_shapes=[pltpu.VMEM((1, LANES), F32)],
        compiler_params=_params(2),
        name="fox_proj",
    )(x, mod, g, w, b_f)


def _fox_attn_kernel(qt_ref, qxt_ref, k_ref, kx_ref, vt_ref, o_ref, s_ref, cmax_ref, p_ref):
    tq = ATTN_TILE
    hd = LANES // 2
    n_blocks = k_ref.shape[0] // tq
    n_slots = s_ref.shape[0]
    chan = lax.broadcasted_iota(jnp.int32, (2 * LANES, 1), 0)
    key = lax.broadcasted_iota(jnp.int32, (tq, tq), 0)
    qry = lax.broadcasted_iota(jnp.int32, (tq, tq), 1)
    causal = key <= qry
    ones = jnp.ones((SUM_ROWS, tq), BF16)
    keep = {}
    for pr in range(ATTN_PAIRS):
        for hh in range(2):
            head = 2 * (pl.program_id(1) * ATTN_PAIRS + pr) + hh
            bias_lo = LANES + BIAS_LANES * head
            keep[(pr, hh)] = (((chan >= hh * hd) & (chan < (hh + 1) * hd))
                              | ((chan >= bias_lo) & (chan < bias_lo + BIAS_LANES)))

    items = [(pr, n, hh, j) for pr in range(ATTN_PAIRS) for n in range(n_blocks)
             for j in [n] + list(range(n)) for hh in range(2)]

    def blk(i):
        return slice(i * tq, (i + 1) * tq)

    def cols(pr):
        return slice(pr * LANES, (pr + 1) * LANES)

    head_q = {}

    def scores(t):
        pr, n, hh, j = items[t]
        if (pr, n, hh) not in head_q:
            q_full = jnp.concatenate([qt_ref[cols(pr), blk(n)], qxt_ref[:, blk(n)]], axis=0)
            head_q[(pr, n, hh)] = jnp.where(keep[(pr, hh)], q_full, jnp.zeros_like(q_full))
        kb = jnp.concatenate([k_ref[blk(j), cols(pr)], kx_ref[blk(j), :]], axis=1)
        s = _dot(kb, head_q[(pr, n, hh)])
        if j == n:
            s = jnp.where(causal, s, -jnp.inf)
        s_ref[t % n_slots] = s
        cmax_ref[t % n_slots] = jnp.max(s, axis=0, keepdims=True)

    run_max = {}
    rescale = {}

    def weights(t):
        grp = items[t][:3]
        m_old = run_max.get(grp)
        m_new = cmax_ref[t % n_slots] if m_old is None else jnp.maximum(m_old, cmax_ref[t % n_slots])
        p_ref[t % n_slots] = jnp.exp2(s_ref[t % n_slots] - m_new).astype(BF16)
        rescale[t] = None if m_old is None else jnp.exp2(m_old - m_new)
        run_max[grp] = m_new

    assert n_slots > PIPE_DEPTH + 1
    for t in range(PIPE_DEPTH):
        scores(t)
    weights(0)
    acc = {}
    for t, (pr, n, hh, j) in enumerate(items):
        if t + PIPE_DEPTH < len(items):
            scores(t + PIPE_DEPTH)
        if t + 1 < len(items):
            weights(t + 1)
        rows = slice(pr * LANES + hh * hd, pr * LANES + (hh + 1) * hd)
        vt = jnp.concatenate([vt_ref[rows, blk(j)], ones], axis=0)
        pv = _dot(vt, p_ref[t % n_slots])
        acc[(pr, n, hh)] = pv if rescale[t] is None else rescale[t] * acc[(pr, n, hh)] + pv
        if j == (n - 1 if n else 0) and hh == 1:
            o_t = [acc[(pr, n, h)][:hd, :] / acc[(pr, n, h)][hd:hd + 1, :] for h in range(2)]
            o_ref[blk(n), cols(pr)] = jnp.concatenate(o_t, axis=0).T.astype(BF16)


def _fox_attn(qt, qxt, k, kx, vt):
    b, s, d = k.shape
    tq = ATTN_TILE
    n_slots = (s // tq) * (s // tq + 1)
    width = ATTN_PAIRS * LANES
    seq_blk = lambda w, col: pl.BlockSpec((None, s, w), col)
    chan_blk = lambda w, col: pl.BlockSpec((None, w, s), col)
    return pl.pallas_call(
        _fox_attn_kernel,
        grid=(b, d // width),
        in_specs=[
            chan_blk(width, lambda bi, p: (bi, p, 0)),
            chan_blk(LANES, lambda bi, p: (bi, 0, 0)),
            seq_blk(width, lambda bi, p: (bi, 0, p)),
            seq_blk(LANES, lambda bi, p: (bi, 0, 0)),
            chan_blk(width, lambda bi, p: (bi, p, 0)),
        ],
        out_specs=seq_blk(width, lambda bi, p: (bi, 0, p)),
        out_shape=jax.ShapeDtypeStruct((b, s, d), BF16),
        scratch_shapes=[pltpu.VMEM((n_slots, tq, tq), F32), pltpu.VMEM((n_slots, 1, tq), F32),
                        pltpu.VMEM((n_slots, tq, tq), BF16)],
        compiler_params=_params(2),
        name="fox_attn",
    )(qt, qxt, k, kx, vt)


def _gla_proj_kernel(x_ref, mod_ref, g_ref, w_ref, wa_ref, wa2_ref, ba_ref, go_ref,
                     qin_ref, kin_ref, kdec_ref, v_ref, gate_ref, dec_ref, b_ref,
                     *, dk, dv, scale, normed):
    tm = x_ref.shape[0]
    h = _input_h(x_ref, mod_ref, g_ref, normed)
    n_chunks = tm // CHUNK
    slab = 2 * dv // n_chunks

    def vr_slab(c):
        vr = _dot(h, w_ref[:, 2 * dk + c * slab:2 * dk + (c + 1) * slab])
        if (c + 1) * slab <= dv:
            v_ref[:, c * slab:(c + 1) * slab] = vr.astype(BF16)
        else:
            cols = slice(c * slab - dv, (c + 1) * slab - dv)
            gate_ref[:, cols] = _silu(vr) * go_ref[:, cols]

    a_lr = _dot(h, wa_ref[...]).astype(BF16)
    qk = _dot(h, w_ref[:, :2 * dk])
    log_alpha = _log_sigmoid(_dot(a_lr, wa2_ref[...]) + ba_ref[...]) / GLA_TAU
    for c in range(SLABS_AHEAD):
        vr_slab(c)

    row = lax.broadcasted_iota(jnp.int32, (CUM_BLOCK, CUM_BLOCK), 0)
    col = lax.broadcasted_iota(jnp.int32, (CUM_BLOCK, CUM_BLOCK), 1)
    tril = jnp.where((row >= col) & (row // CHUNK == col // CHUNK), 1.0, 0.0).astype(BF16)
    terms = jnp.concatenate(_split_bf16(log_alpha, GLA_SPLIT), axis=1)
    for i in range(tm // CUM_BLOCK):
        rows = slice(i * CUM_BLOCK, (i + 1) * CUM_BLOCK)
        sums = _dot(tril, terms[rows, :])
        b_ref[rows, :] = sum(sums[:, t * dk:(t + 1) * dk] for t in range(GLA_SPLIT))

    for c in range(n_chunks):
        if c + SLABS_AHEAD < n_chunks:
            vr_slab(c + SLABS_AHEAD)
        rows = slice(c * CHUNK, (c + 1) * CHUNK)
        b = b_ref[rows, :]
        b_last = b[CHUNK - 1:, :]
        k = qk[rows, dk:2 * dk]
        qin_ref[rows, :] = (qk[rows, :dk] * scale * jnp.exp(b)).astype(BF16)
        kin_ref[rows, :] = (k * jnp.exp(-b)).astype(BF16)
        kdec_ref[rows, :] = (k * jnp.exp(b_last - b)).astype(BF16)
        dec_ref[c:c + 1, :] = jnp.exp(b_last)


def _gla_proj(x, mod, g, w_stack, j, wa, wa2, ba, g_o, layer, normed):
    b, s, d = x.shape
    tm = TOKEN_TILE
    dk = wa2.shape[1]
    dv = g_o.shape[1]
    tok = lambda width: pl.BlockSpec((None, tm, width), lambda bi, si: (bi, si, 0))
    out = lambda width, dt: jax.ShapeDtypeStruct((b, s, width), dt)
    return pl.pallas_call(
        functools.partial(_gla_proj_kernel, dk=dk, dv=dv, scale=(dk // GLA_HEADS) ** -0.5,
                          normed=normed),
        grid=(b, s // tm),
        in_specs=[
            tok(d),
            pl.BlockSpec((None, None, N_MOD, d), lambda bi, si: (layer, bi, 0, 0)),
            pl.BlockSpec((None, 1, d), lambda bi, si: (layer, 0, 0)),
            _layer_of(w_stack, j),
            _resident(wa.shape),
            _resident(wa2.shape),
            _resident(ba.shape),
            _resident(g_o.shape),
        ],
        out_specs=[tok(dk), tok(dk), tok(dk), tok(dv), tok(dv),
                   pl.BlockSpec((None, tm // CHUNK, dk), lambda bi, si: (bi, si, 0))],
        out_shape=[out(dk, BF16), out(dk, BF16), out(dk, BF16), out(dv, BF16), out(dv, F32),
                   jax.ShapeDtypeStruct((b, s // CHUNK, dk), F32)],
        scratch_shapes=[pltpu.VMEM((tm, dk), F32)],
        compiler_params=_params(2),
        name="gla_proj",
    )(x, mod, g, w_stack, wa, wa2, ba, g_o)


def _gla_core_kernel(qin_ref, kin_ref, kdec_ref, v_ref, gate_ref, dec_ref,
                     o_ref, state_ref, kv_ref, a_ref):
    n_seq, tm, dk = qin_ref.shape
    hk = dk // GLA_HEADS
    hv = v_ref.shape[2] // GLA_HEADS

    @pl.when(pl.program_id(1) == 0)
    def _():
        state_ref[...] = jnp.zeros_like(state_ref)

    row = lax.broadcasted_iota(jnp.int32, (CHUNK, CHUNK), 0)
    col = lax.broadcasted_iota(jnp.int32, (CHUNK, CHUNK), 1)
    causal = row >= col

    items = [(c, hd, sq) for c in range(tm // CHUNK) for hd in range(GLA_HEADS)
             for sq in range(n_seq)]

    def span(c, hd):
        return (slice(c * CHUNK, (c + 1) * CHUNK), slice(hd * hk, (hd + 1) * hk),
                slice(hd * hv, (hd + 1) * hv))

    for i, (c, hd, sq) in enumerate(items):
        rows, kc, vc = span(c, hd)
        kv_ref[i] = _dot_tn(v_ref[sq, rows, vc], kdec_ref[sq, rows, kc])
        a = _dot_nt(qin_ref[sq, rows, kc], kin_ref[sq, rows, kc])
        a_ref[i] = jnp.where(causal, a, 0.0).astype(BF16)

    for i, (c, hd, sq) in enumerate(items):
        rows, kc, vc = span(c, hd)
        q = qin_ref[sq, rows, kc]
        state_t = state_ref[sq, hd]
        o = _dot(a_ref[i], v_ref[sq, rows, vc]) + _dot_nt(q, state_t.astype(BF16))
        state_ref[sq, hd] = state_t * dec_ref[sq, c:c + 1, kc] + kv_ref[i]
        o = o * lax.rsqrt(jnp.mean(o * o, axis=-1, keepdims=True) + EPS)
        o_ref[sq, rows, vc] = (o * gate_ref[sq, rows, vc]).astype(BF16)


def _gla_core(qin, kin, kdec, v, gate, dec):
    b, s, dk = qin.shape
    dv = v.shape[2]
    tm = TOKEN_TILE
    nb = GLA_BATCH
    n_items = nb * GLA_HEADS * (tm // CHUNK)
    tok = lambda width: pl.BlockSpec((nb, tm, width), lambda bi, si: (bi, si, 0))
    return pl.pallas_call(
        _gla_core_kernel,
        grid=(b // nb, s // tm),
        in_specs=[tok(dk), tok(dk), tok(dk), tok(dv), tok(dv),
                  pl.BlockSpec((nb, tm // CHUNK, dk), lambda bi, si: (bi, si, 0))],
        out_specs=tok(dv),
        out_shape=jax.ShapeDtypeStruct((b, s, dv), BF16),
        scratch_shapes=[pltpu.VMEM((nb, GLA_HEADS, dv // GLA_HEADS, dk // GLA_HEADS), F32),
                        pltpu.VMEM((n_items, dv // GLA_HEADS, dk // GLA_HEADS), F32),
                        pltpu.VMEM((n_items, CHUNK, CHUNK), BF16)],
        compiler_params=_params(2),
        name="gla_core",
    )(qin, kin, kdec, v, gate, dec)


def _mix_ffn_kernel(x_ref, o_ref, mod_ref, g_ref, wo_ref, win_ref, wout_ref, post_g_ref, post_mod_ref,
                    out_ref, *rest, hidden, final):
    acc_ref = rest[-1]
    tm = x_ref.shape[0]
    n_chunks = hidden // FFN_CHUNK
    parts = [slice(r * tm // ROW_PARTS, (r + 1) * tm // ROW_PARTS) for r in range(ROW_PARTS)]

    for rows in parts:
        out_ref[rows, :] = x_ref[rows, :] + mod_ref[2:3, :] * _dot(o_ref[rows, :], wo_ref[...])
    h = [_rms_mod(out_ref[rows, :], g_ref[...], mod_ref[3:4, :], mod_ref[4:5, :]).astype(BF16)
         for rows in parts]

    def gate_up(t):
        r, j = divmod(t, n_chunks)
        cols = slice(j * FFN_CHUNK, (j + 1) * FFN_CHUNK)
        up_cols = slice(hidden + j * FFN_CHUNK, hidden + (j + 1) * FFN_CHUNK)
        return (_silu(_dot(h[r], win_ref[:, cols])) * _dot(h[r], win_ref[:, up_cols])).astype(BF16)

    def epilogue(rows):
        y = out_ref[rows, :] + mod_ref[5:6, :] * acc_ref[rows, :]
        if final:
            y = y * lax.rsqrt(jnp.mean(y * y, axis=-1, keepdims=True) + EPS) * post_g_ref[...]
        else:
            rest[0][rows, :] = _rms_mod(
                y, post_g_ref[...], post_mod_ref[0:1, :], post_mod_ref[1:2, :]).astype(BF16)
        out_ref[rows, :] = y

    act = gate_up(0)
    for t in range(ROW_PARTS * n_chunks):
        nxt = gate_up(t + 1) if t + 1 < ROW_PARTS * n_chunks else None
        r, j = divmod(t, n_chunks)
        down = _dot(act, wout_ref[j * FFN_CHUNK:(j + 1) * FFN_CHUNK, :])
        if j == 0:
            acc_ref[parts[r], :] = down
        else:
            acc_ref[parts[r], :] += down
        if j == n_chunks - 1:
            epilogue(parts[r])
        act = nxt


def _mix_ffn(x, o, mod, g, wo_stack, j, win_stack, wout_stack, post_g, layer, final):
    b, s, d = x.shape
    tm = TOKEN_TILE
    tok = pl.BlockSpec((None, tm, d), lambda bi, si: (bi, si, 0))
    nxt = layer if final else layer + 1
    post_g_spec = (_resident(post_g.shape) if final
                   else pl.BlockSpec((None, 1, d), lambda bi, si: (nxt, 0, 0)))
    outs = [jax.ShapeDtypeStruct((b, s, d), F32)] + ([] if final else [jax.ShapeDtypeStruct((b, s, d), BF16)])
    res = pl.pallas_call(
        functools.partial(_mix_ffn_kernel, hidden=wout_stack.shape[1], final=final),
        grid=(b, s // tm),
        in_specs=[
            tok, tok,
            pl.BlockSpec((None, None, N_MOD, d), lambda bi, si: (layer, bi, 0, 0)),
            pl.BlockSpec((None, 1, d), lambda bi, si: (layer, 0, 0)),
            _layer_of(wo_stack, j), _layer_of(win_stack, layer), _layer_of(wout_stack, layer),
            post_g_spec,
            pl.BlockSpec((None, None, N_MOD, d), lambda bi, si: (nxt, bi, 0, 0)),
        ],
        out_specs=[tok] * len(outs),
        out_shape=outs,
        scratch_shapes=[pltpu.VMEM((tm, d), F32)],
        compiler_params=_params(2),
        name="mix_ffn",
    )(x, o, mod, g, wo_stack, win_stack, wout_stack, post_g, mod)
    return (res[0], None) if final else (res[0], res[1])


def kernel(x, c, ada_w, ada_b, norm1_g, norm2_g, ffn_w_in, ffn_w_out, fox_w_in, fox_b_f, fox_w_out,
           gla_w_in, gla_w_a2, gla_b_a, gla_g_o, gla_w_out, final_g):
    depth, d, _ = ada_w.shape
    b = x.shape[0]
    assert x.shape[1] % FOX_TILE == 0 and TOKEN_TILE % ATTN_TILE == 0 and FOX_TILE % PART_ROWS == 0
    assert ffn_w_out.shape[1] % FFN_CHUNK == 0
    mod = _adaln_mod(c, ada_w, ada_b).reshape(depth, b, N_MOD, d)
    g1 = norm1_g.reshape(depth, 1, d)
    g2 = norm2_g.reshape(depth, 1, d)
    fg = final_g.reshape(1, d)
    ffn_in, ffn_out = ffn_w_in.astype(BF16), ffn_w_out.astype(BF16)
    fox_in, fox_out = fox_w_in.astype(BF16), fox_w_out.astype(BF16)
    gla_in, gla_out = gla_w_in.astype(BF16), gla_w_out.astype(BF16)
    rank = gla_w_a2.shape[1]
    h = None
    for i in range(depth):
        j = i // 2
        src, normed = (x, False) if h is None else (h, True)
        if i % 2 == 0:
            w_f = fox_in[j][:, 3 * d:]
            pad = jnp.zeros((d, LANES - N_SPLIT * FOX_HEADS), BF16)
            w = jnp.concatenate([fox_in[j][:, :3 * d]] + [w_f] * N_SPLIT + [pad], axis=1)
            b_f = jnp.concatenate(
                [fox_b_f[j]] * N_SPLIT + [jnp.zeros((LANES - N_SPLIT * FOX_HEADS,), F32)])
            qt, k, vt, qxt, kx = _fox_proj(src, mod, g1, w, b_f.reshape(1, LANES), i, normed)
            o = _fox_attn(qt, qxt, k, kx, vt)
            w_o = fox_out
        else:
            n_main = gla_in.shape[2] - rank
            wa = jnp.pad(gla_in[j][:, n_main:], ((0, 0), (0, LANES - rank)))
            wa2 = jnp.pad(gla_w_a2[j], ((0, LANES - rank), (0, 0))).astype(BF16)
            qin, kin, kdec, v, gate, dec = _gla_proj(
                src, mod, g1, gla_in, j, wa, wa2, gla_b_a[j].reshape(1, -1),
                gla_g_o[j].reshape(1, -1), i, normed)
            o = _gla_core(qin, kin, kdec, v, gate, dec)
            w_o = gla_out
        final = i == depth - 1
        x, h = _mix_ffn(x, o, mod, g2, w_o, j, ffn_in, ffn_out, fg if final else g1, i, final)
    return x
```

```python
import functools
import math

import jax
import jax.numpy as jnp
from jax import lax
from jax.experimental import pallas as pl
from jax.experimental.pallas import tpu as pltpu

F32 = jnp.float32
BF16 = jnp.bfloat16

EPS = 1e-6
LOG2_E = math.log2(math.e)
N_MOD = 6
CHUNK = 64
FOX_HEADS = 16
GLA_HEADS = 4
GLA_TAU = 16.0
LANES = 128
BIAS_LANES = 8
N_SPLIT = 3
GLA_SPLIT = 2
SUM_ROWS = 16

TOKEN_TILE = 512
ATTN_TILE = 256
ATTN_PAIRS = 2
PIPE_DEPTH = 6
GLA_BATCH = 2
SLABS_AHEAD = 3
FFN_CHUNK = 256
ROW_PARTS = 2
FOX_TILE = 1024
GLA_TILE = 1024
SLAB_COLS = 256
PART_ROWS = 256
CUM_BLOCK = 128
VMEM_LIMIT = 56 * 1024 * 1024


def _params(n_grid, flags=None):
    return pltpu.CompilerParams(
        dimension_semantics=("arbitrary",) * n_grid, vmem_limit_bytes=VMEM_LIMIT, flags=flags)


def _resident(shape):
    return pl.BlockSpec(shape, lambda *_: (0,) * len(shape), pipeline_mode=pl.Buffered(1))


def _layer_of(stack, idx):
    return pl.BlockSpec((None,) + stack.shape[1:], lambda *_: (idx, 0, 0),
                        pipeline_mode=pl.Buffered(1))


def _rms_mod(x, g, shift, scale):
    y = x * lax.rsqrt(jnp.mean(x * x, axis=-1, keepdims=True) + EPS) * g
    return y * (1.0 + scale) + shift


def _input_h(x_ref, mod_ref, g_ref, normed):
    if normed:
        return x_ref[...]
    return _rms_mod(x_ref[...], g_ref[...], mod_ref[0:1, :], mod_ref[1:2, :]).astype(BF16)


def _log_sigmoid(x):
    return jnp.minimum(x, 0.0) - jnp.log(1.0 + jnp.exp(-jnp.abs(x)))


def _silu(x):
    return x * jax.nn.sigmoid(x)


def _split_bf16(x, n_terms=N_SPLIT):
    terms = []
    for _ in range(n_terms - 1):
        t = x.astype(BF16)
        terms.append(t)
        x = x - t.astype(F32)
    terms.append(x.astype(BF16))
    return terms


def _dot(a, b):
    return jnp.dot(a, b, preferred_element_type=F32)


def _dot_nt(a, b):
    return lax.dot_general(a, b, (((1,), (1,)), ((), ())), preferred_element_type=F32)


def _dot_tn(a, b):
    return lax.dot_general(a, b, (((0,), (0,)), ((), ())), preferred_element_type=F32)


def _adaln_kernel(c_ref, w_ref, b_ref, o_ref):
    ca = _silu(c_ref[...]).astype(BF16)
    o_ref[...] = _dot(ca, w_ref[...].astype(BF16)) + b_ref[...]


def _adaln_mod(c, ada_w, ada_b):
    depth, d, n = ada_w.shape
    b = c.shape[0]
    bn = 1024
    return pl.pallas_call(
        _adaln_kernel,
        grid=(depth, n // bn),
        in_specs=[
            pl.BlockSpec((b, d), lambda i, j: (0, 0)),
            pl.BlockSpec((None, d, bn), lambda i, j: (i, 0, j)),
            pl.BlockSpec((None, 1, bn), lambda i, j: (i, 0, j)),
        ],
        out_specs=pl.BlockSpec((None, b, bn), lambda i, j: (i, 0, j)),
        out_shape=jax.ShapeDtypeStruct((depth, b, n), F32),
        compiler_params=_params(2),
        name="adaln_mod",
    )(c, ada_w, ada_b.reshape(depth, 1, n))


def _fox_proj_kernel(x_ref, mod_ref, g_ref, w_ref, bf_ref,
                     qt_ref, k_ref, vt_ref, qxt_ref, kx_ref, carry_ref, *, d, scale, normed):
    tm = x_ref.shape[0]
    part = PART_ROWS
    parts = [slice(r * part, (r + 1) * part) for r in range(tm // part)]

    @pl.when(pl.program_id(1) == 0)
    def _():
        carry_ref[...] = jnp.zeros_like(carry_ref)

    lane = lax.broadcasted_iota(jnp.int32, (1, LANES), 1)
    used = lane < N_SPLIT * FOX_HEADS
    group = lane // FOX_HEADS
    row = lax.broadcasted_iota(jnp.int32, (part, part), 0)
    col = lax.broadcasted_iota(jnp.int32, (part, part), 1)
    tril = jnp.where(row >= col, 1.0, 0.0).astype(BF16)
    src = lax.broadcasted_iota(jnp.int32, (LANES, LANES), 0)
    dst = lax.broadcasted_iota(jnp.int32, (LANES, LANES), 1)
    same_head = (src % FOX_HEADS == dst // BIAS_LANES) & (src < N_SPLIT * FOX_HEADS)
    place_q = jnp.where(same_head & (src // FOX_HEADS == dst % BIAS_LANES), 1.0, 0.0).astype(BF16)
    place_k = jnp.where(same_head & (src // FOX_HEADS + N_SPLIT == dst % BIAS_LANES), 1.0, 0.0).astype(BF16)
    pos = lane % BIAS_LANES
    ones_q = jnp.where((pos >= N_SPLIT) & (pos < 2 * N_SPLIT), 1.0, 0.0)
    ones_k = jnp.where(pos < N_SPLIT, 1.0, 0.0)

    h = [_input_h(x_ref.at[rows, :], mod_ref, g_ref, normed) for rows in parts]
    proj = [_dot(hp, w_ref[...]) for hp in h]
    log_f = []
    for rows, pr in zip(parts, proj):
        qt_ref[:, rows] = (pr[:, :d] * (scale * LOG2_E)).T.astype(BF16)
        k_ref[rows, :] = pr[:, d:2 * d].astype(BF16)
        vt_ref[:, rows] = pr[:, 2 * d:3 * d].T.astype(BF16)
        log_f.append(jnp.where(used, _log_sigmoid(pr[:, 3 * d:] + bf_ref[...]), 0.0))
    sums = [_dot(tril, jnp.concatenate(_split_bf16(lf), axis=1)) for lf in log_f]
    carry = carry_ref[...]
    for rows, sm in zip(parts, sums):
        cum = carry + sum(sm[:, t * LANES:(t + 1) * LANES] for t in range(N_SPLIT))
        carry = cum[part - 1:part, :]
        terms = _split_bf16(cum * LOG2_E)
        packed = jnp.zeros_like(terms[0])
        for t in range(N_SPLIT):
            packed = jnp.where(group == t, terms[t], packed)
        qxt_ref[:, rows] = (_dot(packed, place_q) + ones_q).T.astype(BF16)
        kx_ref[rows, :] = (ones_k - _dot(packed, place_k)).astype(BF16)
    carry_ref[...] = carry


def _fox_proj(x, mod, g, w, b_f, layer, normed):
    b, s, d = x.shape
    tm = FOX_TILE
    tok = lambda width: pl.BlockSpec((None, tm, width), lambda bi, si: (bi, si, 0))
    tr = lambda width: pl.BlockSpec((None, width, tm), lambda bi, si: (bi, 0, si))
    out_bf16 = lambda width: jax.ShapeDtypeStruct((b, s, width), BF16)
    out_t = lambda width: jax.ShapeDtypeStruct((b, width, s), BF16)
    return pl.pallas_call(
        functools.partial(_fox_proj_kernel, d=d, scale=(d // FOX_HEADS) ** -0.5, normed=normed),
        grid=(b, s // tm),
        in_specs=[
            tok(d),
            pl.BlockSpec((None, None, N_MOD, d), lambda bi, si: (layer, bi, 0, 0)),
            pl.BlockSpec((None, 1, d), lambda bi, si: (layer, 0, 0)),
            _resident(w.shape),
            _resident(b_f.shape),
        ],
        out_specs=[tr(d), tok(d), tr(d), tr(LANES), tok(LANES)],
        out_shape=[out_t(d), out_bf16(d), out_t(d), out_t(LANES), out_bf16(LANES)],
        scratch_shapes=[pltpu.VMEM((1, LANES), F32)],
        compiler_params=_params(2),
        name="fox_proj",
    )(x, mod, g, w, b_f)


def _fox_attn_kernel(qt_ref, qxt_ref, k_ref, kx_ref, vt_ref, o_ref, s_ref, cmax_ref, p_ref):
    tq = ATTN_TILE
    hd = LANES // 2
    n_blocks = k_ref.shape[0] // tq
    n_slots = s_ref.shape[0]
    chan = lax.broadcasted_iota(jnp.int32, (2 * LANES, 1), 0)
    key = lax.broadcasted_iota(jnp.int32, (tq, tq), 0)
    qry = lax.broadcasted_iota(jnp.int32, (tq, tq), 1)
    causal = key <= qry
    ones = jnp.ones((SUM_ROWS, tq), BF16)
    keep = {}
    for pr in range(ATTN_PAIRS):
        for hh in range(2):
            head = 2 * (pl.program_id(1) * ATTN_PAIRS + pr) + hh
            bias_lo = LANES + BIAS_LANES * head
            keep[(pr, hh)] = (((chan >= hh * hd) & (chan < (hh + 1) * hd))
                              | ((chan >= bias_lo) & (chan < bias_lo + BIAS_LANES)))

    items = [(pr, n, hh, j) for pr in range(ATTN_PAIRS) for n in range(n_blocks)
             for j in [n] + list(range(n)) for hh in range(2)]

    def blk(i):
        return slice(i * tq, (i + 1) * tq)

    def cols(pr):
        return slice(pr * LANES, (pr + 1) * LANES)

    head_q = {}

    def scores(t):
        pr, n, hh, j = items[t]
        if (pr, n, hh) not in head_q:
            q_full = jnp.concatenate([qt_ref[cols(pr), blk(n)], qxt_ref[:, blk(n)]], axis=0)
            head_q[(pr, n, hh)] = jnp.where(keep[(pr, hh)], q_full, jnp.zeros_like(q_full))
        kb = jnp.concatenate([k_ref[blk(j), cols(pr)], kx_ref[blk(j), :]], axis=1)
        s = _dot(kb, head_q[(pr, n, hh)])
        if j == n:
            s = jnp.where(causal, s, -jnp.inf)
        s_ref[t % n_slots] = s
        cmax_ref[t % n_slots] = jnp.max(s, axis=0, keepdims=True)

    run_max = {}
    rescale = {}

    def weights(t):
        grp = items[t][:3]
        m_old = run_max.get(grp)
        m_new = cmax_ref[t % n_slots] if m_old is None else jnp.maximum(m_old, cmax_ref[t % n_slots])
        p_ref[t % n_slots] = jnp.exp2(s_ref[t % n_slots] - m_new).astype(BF16)
        rescale[t] = None if m_old is None else jnp.exp2(m_old - m_new)
        run_max[grp] = m_new

    assert n_slots > PIPE_DEPTH + 1
    for t in range(PIPE_DEPTH):
        scores(t)
    weights(0)
    acc = {}
    for t, (pr, n, hh, j) in enumerate(items):
        if t + PIPE_DEPTH < len(items):
            scores(t + PIPE_DEPTH)
        if t + 1 < len(items):
            weights(t + 1)
        rows = slice(pr * LANES + hh * hd, pr * LANES + (hh + 1) * hd)
        vt = jnp.concatenate([vt_ref[rows, blk(j)], ones], axis=0)
        pv = _dot(vt, p_ref[t % n_slots])
        acc[(pr, n, hh)] = pv if rescale[t] is None else rescale[t] * acc[(pr, n, hh)] + pv
        if j == (n - 1 if n else 0) and hh == 1:
            o_t = [acc[(pr, n, h)][:hd, :] / acc[(pr, n, h)][hd:hd + 1, :] for h in range(2)]
            o_ref[blk(n), cols(pr)] = jnp.concatenate(o_t, axis=0).T.astype(BF16)


def _fox_attn(qt, qxt, k, kx, vt):
    b, s, d = k.shape
    tq = ATTN_TILE
    n_slots = (s // tq) * (s // tq + 1)
    width = ATTN_PAIRS * LANES
    seq_blk = lambda w, col: pl.BlockSpec((None, s, w), col)
    chan_blk = lambda w, col: pl.BlockSpec((None, w, s), col)
    return pl.pallas_call(
        _fox_attn_kernel,
        grid=(b, d // width),
        in_specs=[
            chan_blk(width, lambda bi, p: (bi, p, 0)),
            chan_blk(LANES, lambda bi, p: (bi, 0, 0)),
            seq_blk(width, lambda bi, p: (bi, 0, p)),
            seq_blk(LANES, lambda bi, p: (bi, 0, 0)),
            chan_blk(width, lambda bi, p: (bi, p, 0)),
        ],
        out_specs=seq_blk(width, lambda bi, p: (bi, 0, p)),
        out_shape=jax.ShapeDtypeStruct((b, s, d), BF16),
        scratch_shapes=[pltpu.VMEM((n_slots, tq, tq), F32), pltpu.VMEM((n_slots, 1, tq), F32),
                        pltpu.VMEM((n_slots, tq, tq), BF16)],
        compiler_params=_params(2),
        name="fox_attn",
    )(qt, qxt, k, kx, vt)


def _gla_proj_kernel(x_ref, mod_ref, g_ref, w_ref, wa_ref, wa2_ref, ba_ref, go_ref,
                     qin_ref, kin_ref, kdec_ref, v_ref, gate_ref, dec_ref, b_ref,
                     *, dk, dv, scale, normed):
    tm = x_ref.shape[0]
    h = _input_h(x_ref, mod_ref, g_ref, normed)
    n_chunks = tm // CHUNK
    slab = SLAB_COLS
    n_slabs = 2 * dv // slab
    every = n_chunks // n_slabs

    def vr_slab(c):
        vr = _dot(h, w_ref[:, 2 * dk + c * slab:2 * dk + (c + 1) * slab])
        if (c + 1) * slab <= dv:
            v_ref[:, c * slab:(c + 1) * slab] = vr.astype(BF16)
        else:
            cols = slice(c * slab - dv, (c + 1) * slab - dv)
            gate_ref[:, cols] = _silu(vr) * go_ref[:, cols]

    a_lr = _dot(h, wa_ref[...]).astype(BF16)
    qk = _dot(h, w_ref[:, :2 * dk])
    log_alpha = _log_sigmoid(_dot(a_lr, wa2_ref[...]) + ba_ref[...]) / GLA_TAU
    for c in range(SLABS_AHEAD):
        vr_slab(c)

    row = lax.broadcasted_iota(jnp.int32, (CUM_BLOCK, CUM_BLOCK), 0)
    col = lax.broadcasted_iota(jnp.int32, (CUM_BLOCK, CUM_BLOCK), 1)
    tril = jnp.where((row >= col) & (row // CHUNK == col // CHUNK), 1.0, 0.0).astype(BF16)
    terms = jnp.concatenate(_split_bf16(log_alpha, GLA_SPLIT), axis=1)
    for i in range(tm // CUM_BLOCK):
        rows = slice(i * CUM_BLOCK, (i + 1) * CUM_BLOCK)
        sums = _dot(tril, terms[rows, :])
        b_ref[rows, :] = sum(sums[:, t * dk:(t + 1) * dk] for t in range(GLA_SPLIT))

    for c in range(n_chunks):
        if c % every == 0 and c // every + SLABS_AHEAD < n_slabs:
            vr_slab(c // every + SLABS_AHEAD)
        rows = slice(c * CHUNK, (c + 1) * CHUNK)
        b = b_ref[rows, :]
        b_last = b[CHUNK - 1:, :]
        k = qk[rows, dk:2 * dk]
        qin_ref[rows, :] = (qk[rows, :dk] * scale * jnp.exp(b)).astype(BF16)
        kin_ref[rows, :] = (k * jnp.exp(-b)).astype(BF16)
        kdec_ref[rows, :] = (k * jnp.exp(b_last - b)).astype(BF16)
        dec_ref[c:c + 1, :] = jnp.exp(b_last)


def _gla_proj(x, mod, g, w_stack, j, wa, wa2, ba, g_o, layer, normed):
    b, s, d = x.shape
    tm = GLA_TILE
    dk = wa2.shape[1]
    dv = g_o.shape[1]
    tok = lambda width: pl.BlockSpec((None, tm, width), lambda bi, si: (bi, si, 0))
    out = lambda width, dt: jax.ShapeDtypeStruct((b, s, width), dt)
    return pl.pallas_call(
        functools.partial(_gla_proj_kernel, dk=dk, dv=dv, scale=(dk // GLA_HEADS) ** -0.5,
                          normed=normed),
        grid=(b, s // tm),
        in_specs=[
            tok(d),
            pl.BlockSpec((None, None, N_MOD, d), lambda bi, si: (layer, bi, 0, 0)),
            pl.BlockSpec((None, 1, d), lambda bi, si: (layer, 0, 0)),
            _layer_of(w_stack, j),
            _resident(wa.shape),
            _resident(wa2.shape),
            _resident(ba.shape),
            _resident(g_o.shape),
        ],
        out_specs=[tok(dk), tok(dk), tok(dk), tok(dv), tok(dv),
                   pl.BlockSpec((None, tm // CHUNK, dk), lambda bi, si: (bi, si, 0))],
        out_shape=[out(dk, BF16), out(dk, BF16), out(dk, BF16), out(dv, BF16), out(dv, F32),
                   jax.ShapeDtypeStruct((b, s // CHUNK, dk), F32)],
        scratch_shapes=[pltpu.VMEM((tm, dk), F32)],
        compiler_params=_params(2),
        name="gla_proj",
    )(x, mod, g, w_stack, wa, wa2, ba, g_o)


def _gla_core_kernel(qin_ref, kin_ref, kdec_ref, v_ref, gate_ref, dec_ref,
                     o_ref, state_ref, kv_ref, a_ref):
    n_seq, tm, dk = qin_ref.shape
    hk = dk // GLA_HEADS
    hv = v_ref.shape[2] // GLA_HEADS

    @pl.when(pl.program_id(1) == 0)
    def _():
        state_ref[...] = jnp.zeros_like(state_ref)

    row = lax.broadcasted_iota(jnp.int32, (CHUNK, CHUNK), 0)
    col = lax.broadcasted_iota(jnp.int32, (CHUNK, CHUNK), 1)
    causal = row >= col

    items = [(c, hd, sq) for c in range(tm // CHUNK) for hd in range(GLA_HEADS)
             for sq in range(n_seq)]

    def span(c, hd):
        return (slice(c * CHUNK, (c + 1) * CHUNK), slice(hd * hk, (hd + 1) * hk),
                slice(hd * hv, (hd + 1) * hv))

    for i, (c, hd, sq) in enumerate(items):
        rows, kc, vc = span(c, hd)
        kv_ref[i] = _dot_tn(v_ref[sq, rows, vc], kdec_ref[sq, rows, kc])
        a = _dot_nt(qin_ref[sq, rows, kc], kin_ref[sq, rows, kc])
        a_ref[i] = jnp.where(causal, a, 0.0).astype(BF16)

    for i, (c, hd, sq) in enumerate(items):
        rows, kc, vc = span(c, hd)
        q = qin_ref[sq, rows, kc]
        state_t = state_ref[sq, hd]
        o = _dot(a_ref[i], v_ref[sq, rows, vc]) + _dot_nt(q, state_t.astype(BF16))
        state_ref[sq, hd] = state_t * dec_ref[sq, c:c + 1, kc] + kv_ref[i]
        o = o * lax.rsqrt(jnp.mean(o * o, axis=-1, keepdims=True) + EPS)
        o_ref[sq, rows, vc] = (o * gate_ref[sq, rows, vc]).astype(BF16)


def _gla_core(qin, kin, kdec, v, gate, dec):
    b, s, dk = qin.shape
    dv = v.shape[2]
    tm = TOKEN_TILE
    nb = GLA_BATCH
    n_items = nb * GLA_HEADS * (tm // CHUNK)
    tok = lambda width: pl.BlockSpec((nb, tm, width), lambda bi, si: (bi, si, 0))
    return pl.pallas_call(
        _gla_core_kernel,
        grid=(b // nb, s // tm),
        in_specs=[tok(dk), tok(dk), tok(dk), tok(dv), tok(dv),
                  pl.BlockSpec((nb, tm // CHUNK, dk), lambda bi, si: (bi, si, 0))],
        out_specs=tok(dv),
        out_shape=jax.ShapeDtypeStruct((b, s, dv), BF16),
        scratch_shapes=[pltpu.VMEM((nb, GLA_HEADS, dv // GLA_HEADS, dk // GLA_HEADS), F32),
                        pltpu.VMEM((n_items, dv // GLA_HEADS, dk // GLA_HEADS), F32),
                        pltpu.VMEM((n_items, CHUNK, CHUNK), BF16)],
        compiler_params=_params(2),
        name="gla_core",
    )(qin, kin, kdec, v, gate, dec)


def _mix_ffn_kernel(x_ref, o_ref, mod_ref, g_ref, wo_ref, win_ref, wout_ref, post_g_ref, post_mod_ref,
                    out_ref, *rest, hidden, final):
    acc_ref = rest[-1]
    tm = x_ref.shape[0]
    n_chunks = hidden // FFN_CHUNK
    parts = [slice(r * tm // ROW_PARTS, (r + 1) * tm // ROW_PARTS) for r in range(ROW_PARTS)]

    for rows in parts:
        out_ref[rows, :] = x_ref[rows, :] + mod_ref[2:3, :] * _dot(o_ref[rows, :], wo_ref[...])
    h = [_rms_mod(out_ref[rows, :], g_ref[...], mod_ref[3:4, :], mod_ref[4:5, :]).astype(BF16)
         for rows in parts]

    def gate_up(t):
        r, j = divmod(t, n_chunks)
        cols = slice(j * FFN_CHUNK, (j + 1) * FFN_CHUNK)
        up_cols = slice(hidden + j * FFN_CHUNK, hidden + (j + 1) * FFN_CHUNK)
        return (_silu(_dot(h[r], win_ref[:, cols])) * _dot(h[r], win_ref[:, up_cols])).astype(BF16)

    def epilogue(rows):
        y = out_ref[rows, :] + mod_ref[5:6, :] * acc_ref[rows, :]
        if final:
            y = y * lax.rsqrt(jnp.mean(y * y, axis=-1, keepdims=True) + EPS) * post_g_ref[...]
        else:
            rest[0][rows, :] = _rms_mod(
                y, post_g_ref[...], post_mod_ref[0:1, :], post_mod_ref[1:2, :]).astype(BF16)
        out_ref[rows, :] = y

    act = gate_up(0)
    for t in range(ROW_PARTS * n_chunks):
        nxt = gate_up(t + 1) if t + 1 < ROW_PARTS * n_chunks else None
        r, j = divmod(t, n_chunks)
        down = _dot(act, wout_ref[j * FFN_CHUNK:(j + 1) * FFN_CHUNK, :])
        if j == 0:
            acc_ref[parts[r], :] = down
        else:
            acc_ref[parts[r], :] += down
        if j == n_chunks - 1:
            epilogue(parts[r])
        act = nxt


def _mix_ffn(x, o, mod, g, wo_stack, j, win_stack, wout_stack, post_g, layer, final):
    b, s, d = x.shape
    tm = TOKEN_TILE
    tok = pl.BlockSpec((None, tm, d), lambda bi, si: (bi, si, 0))
    nxt = layer if final else layer + 1
    post_g_spec = (_resident(post_g.shape) if final
                   else pl.BlockSpec((None, 1, d), lambda bi, si: (nxt, 0, 0)))
    outs = [jax.ShapeDtypeStruct((b, s, d), F32)] + ([] if final else [jax.ShapeDtypeStruct((b, s, d), BF16)])
    res = pl.pallas_call(
        functools.partial(_mix_ffn_kernel, hidden=wout_stack.shape[1], final=final),
        grid=(b, s // tm),
        in_specs=[
            tok, tok,
            pl.BlockSpec((None, None, N_MOD, d), lambda bi, si: (layer, bi, 0, 0)),
            pl.BlockSpec((None, 1, d), lambda bi, si: (layer, 0, 0)),
            _layer_of(wo_stack, j), _layer_of(win_stack, layer), _layer_of(wout_stack, layer),
            post_g_spec,
            pl.BlockSpec((None, None, N_MOD, d), lambda bi, si: (nxt, bi, 0, 0)),
        ],
        out_specs=[tok] * len(outs),
        out_shape=outs,
        scratch_shapes=[pltpu.VMEM((tm, d), F32)],
        compiler_params=_params(2),
        name="mix_ffn",
    )(x, o, mod, g, wo_stack, win_stack, wout_stack, post_g, mod)
    return (res[0], None) if final else (res[0], res[1])


def kernel(x, c, ada_w, ada_b, norm1_g, norm2_g, ffn_w_in, ffn_w_out, fox_w_in, fox_b_f, fox_w_out,
           gla_w_in, gla_w_a2, gla_b_a, gla_g_o, gla_w_out, final_g):
    depth, d, _ = ada_w.shape
    b = x.shape[0]
    assert x.shape[1] % FOX_TILE == 0 and x.shape[1] % GLA_TILE == 0 and TOKEN_TILE % ATTN_TILE == 0 and FOX_TILE % PART_ROWS == 0
    assert ffn_w_out.shape[1] % FFN_CHUNK == 0
    mod = _adaln_mod(c, ada_w, ada_b).reshape(depth, b, N_MOD, d)
    g1 = norm1_g.reshape(depth, 1, d)
    g2 = norm2_g.reshape(depth, 1, d)
    fg = final_g.reshape(1, d)
    ffn_in, ffn_out = ffn_w_in.astype(BF16), ffn_w_out.astype(BF16)
    fox_in, fox_out = fox_w_in.astype(BF16), fox_w_out.astype(BF16)
    gla_in, gla_out = gla_w_in.astype(BF16), gla_w_out.astype(BF16)
    rank = gla_w_a2.shape[1]
    h = None
    for i in range(depth):
        j = i // 2
        src, normed = (x, False) if h is None else (h, True)
        if i % 2 == 0:
            w_f = fox_in[j][:, 3 * d:]
            pad = jnp.zeros((d, LANES - N_SPLIT * FOX_HEADS), BF16)
            w = jnp.concatenate([fox_in[j][:, :3 * d]] + [w_f] * N_SPLIT + [pad], axis=1)
            b_f = jnp.concatenate(
                [fox_b_f[j]] * N_SPLIT + [jnp.zeros((LANES - N_SPLIT * FOX_HEADS,), F32)])
            qt, k, vt, qxt, kx = _fox_proj(src, mod, g1, w, b_f.reshape(1, LANES), i, normed)
            o = _fox_attn(qt, qxt, k, kx, vt)
            w_o = fox_out
        else:
            n_main = gla_in.shape[2] - rank
            wa = jnp.pad(gla_in[j][:, n_main:], ((0, 0), (0, LANES - rank)))
            wa2 = jnp.pad(gla_w_a2[j], ((0, LANES - rank), (0, 0))).astype(BF16)
            qin, kin, kdec, v, gate, dec = _gla_proj(
                src, mod, g1, gla_in, j, wa, wa2, gla_b_a[j].reshape(1, -1),
                gla_g_o[j].reshape(1, -1), i, normed)
            o = _gla_core(qin, kin, kdec, v, gate, dec)
            w_o = gla_out
        final = i == depth - 1
        x, h = _mix_ffn(x, o, mod, g2, w_o, j, ffn_in, ffn_out, fg if final else g1, i, final)
    return x
```

```python
import functools
import math

import jax
import jax.numpy as jnp
from jax import lax
from jax.experimental import pallas as pl
from jax.experimental.pallas import tpu as pltpu

F32 = jnp.float32
BF16 = jnp.bfloat16

EPS = 1e-6
LOG2_E = math.log2(math.e)
N_MOD = 6
CHUNK = 64
FOX_HEADS = 16
GLA_HEADS = 4
GLA_TAU = 16.0
LANES = 128
BIAS_LANES = 8
N_SPLIT = 3
GLA_SPLIT = 2
SUM_ROWS = 16

TOKEN_TILE = 512
ATTN_TILE = 256
ATTN_PAIRS = 2
PIPE_DEPTH = 6
GLA_BATCH = 2
SLABS_AHEAD = 3
FFN_CHUNK = 256
ROW_PARTS = 2
FOX_TILE = 1024
GLA_TILE = 1024
SLAB_COLS = 256
PART_ROWS = 256
CUM_BLOCK = 128
VMEM_LIMIT = 56 * 1024 * 1024


def _params(n_grid, flags=None):
    return pltpu.CompilerParams(
        dimension_semantics=("arbitrary",) * n_grid, vmem_limit_bytes=VMEM_LIMIT, flags=flags)


def _resident(shape):
    return pl.BlockSpec(shape, lambda *_: (0,) * len(shape), pipeline_mode=pl.Buffered(1))


def _layer_of(stack, idx):
    return pl.BlockSpec((None,) + stack.shape[1:], lambda *_: (idx, 0, 0),
                        pipeline_mode=pl.Buffered(1))


def _rms_mod(x, g, shift, scale):
    y = x * lax.rsqrt(jnp.mean(x * x, axis=-1, keepdims=True) + EPS) * g
    return y * (1.0 + scale) + shift


def _input_h(x_ref, mod_ref, g_ref, normed):
    if normed:
        return x_ref[...]
    return _rms_mod(x_ref[...], g_ref[...], mod_ref[0:1, :], mod_ref[1:2, :]).astype(BF16)


def _log_sigmoid(x):
    return jnp.minimum(x, 0.0) - jnp.log(1.0 + jnp.exp(-jnp.abs(x)))


def _silu(x):
    return x * jax.nn.sigmoid(x)


def _split_bf16(x, n_terms=N_SPLIT):
    terms = []
    for _ in range(n_terms - 1):
        t = x.astype(BF16)
        terms.append(t)
        x = x - t.astype(F32)
    terms.append(x.astype(BF16))
    return terms


def _dot(a, b):
    return jnp.dot(a, b, preferred_element_type=F32)


def _dot_nt(a, b):
    return lax.dot_general(a, b, (((1,), (1,)), ((), ())), preferred_element_type=F32)


def _dot_tn(a, b):
    return lax.dot_general(a, b, (((0,), (0,)), ((), ())), preferred_element_type=F32)


def _adaln_kernel(c_ref, w_ref, b_ref, o_ref):
    ca = _silu(c_ref[...]).astype(BF16)
    o_ref[...] = _dot(ca, w_ref[...].astype(BF16)) + b_ref[...]


def _adaln_mod(c, ada_w, ada_b):
    depth, d, n = ada_w.shape
    b = c.shape[0]
    bn = 3072
    return pl.pallas_call(
        _adaln_kernel,
        grid=(depth, n // bn),
        in_specs=[
            pl.BlockSpec((b, d), lambda i, j: (0, 0)),
            pl.BlockSpec((None, d, bn), lambda i, j: (i, 0, j)),
            pl.BlockSpec((None, 1, bn), lambda i, j: (i, 0, j)),
        ],
        out_specs=pl.BlockSpec((None, b, bn), lambda i, j: (i, 0, j)),
        out_shape=jax.ShapeDtypeStruct((depth, b, n), F32),
        compiler_params=_params(2),
        name="adaln_mod",
    )(c, ada_w, ada_b.reshape(depth, 1, n))


def _fox_proj_kernel(x_ref, mod_ref, g_ref, w_ref, bf_ref,
                     qt_ref, k_ref, vt_ref, qxt_ref, kx_ref, carry_ref, *, d, scale, normed):
    tm = x_ref.shape[0]
    part = PART_ROWS
    parts = [slice(r * part, (r + 1) * part) for r in range(tm // part)]

    @pl.when(pl.program_id(1) == 0)
    def _():
        carry_ref[...] = jnp.zeros_like(carry_ref)

    lane = lax.broadcasted_iota(jnp.int32, (1, LANES), 1)
    used = lane < N_SPLIT * FOX_HEADS
    group = lane // FOX_HEADS
    row = lax.broadcasted_iota(jnp.int32, (part, part), 0)
    col = lax.broadcasted_iota(jnp.int32, (part, part), 1)
    tril = jnp.where(row >= col, 1.0, 0.0).astype(BF16)
    src = lax.broadcasted_iota(jnp.int32, (LANES, LANES), 0)
    dst = lax.broadcasted_iota(jnp.int32, (LANES, LANES), 1)
    same_head = (src % FOX_HEADS == dst // BIAS_LANES) & (src < N_SPLIT * FOX_HEADS)
    place_q = jnp.where(same_head & (src // FOX_HEADS == dst % BIAS_LANES), 1.0, 0.0).astype(BF16)
    place_k = jnp.where(same_head & (src // FOX_HEADS + N_SPLIT == dst % BIAS_LANES), 1.0, 0.0).astype(BF16)
    pos = lane % BIAS_LANES
    ones_q = jnp.where((pos >= N_SPLIT) & (pos < 2 * N_SPLIT), 1.0, 0.0)
    ones_k = jnp.where(pos < N_SPLIT, 1.0, 0.0)

    h = [_input_h(x_ref.at[rows, :], mod_ref, g_ref, normed) for rows in parts]
    proj = [_dot(hp, w_ref[...]) for hp in h]
    log_f = []
    for rows, pr in zip(parts, proj):
        qt_ref[:, rows] = (pr[:, :d] * (scale * LOG2_E)).T.astype(BF16)
        k_ref[rows, :] = pr[:, d:2 * d].astype(BF16)
        vt_ref[:, rows] = pr[:, 2 * d:3 * d].T.astype(BF16)
        log_f.append(jnp.where(used, _log_sigmoid(pr[:, 3 * d:] + bf_ref[...]), 0.0))
    sums = [_dot(tril, jnp.concatenate(_split_bf16(lf), axis=1)) for lf in log_f]
    carry = carry_ref[...]
    for rows, sm in zip(parts, sums):
        cum = carry + sum(sm[:, t * LANES:(t + 1) * LANES] for t in range(N_SPLIT))
        carry = cum[part - 1:part, :]
        terms = _split_bf16(cum * LOG2_E)
        packed = jnp.zeros_like(terms[0])
        for t in range(N_SPLIT):
            packed = jnp.where(group == t, terms[t], packed)
        qxt_ref[:, rows] = (_dot(packed, place_q) + ones_q).T.astype(BF16)
        kx_ref[rows, :] = (ones_k - _dot(packed, place_k)).astype(BF16)
    carry_ref[...] = carry


def _fox_proj(x, mod, g, w, b_f, layer, normed):
    b, s, d = x.shape
    tm = FOX_TILE
    tok = lambda width: pl.BlockSpec((None, tm, width), lambda bi, si: (bi, si, 0))
    tr = lambda width: pl.BlockSpec((None, width, tm), lambda bi, si: (bi, 0, si))
    out_bf16 = lambda width: jax.ShapeDtypeStruct((b, s, width), BF16)
    out_t = lambda width: jax.ShapeDtypeStruct((b, width, s), BF16)
    return pl.pallas_call(
        functools.partial(_fox_proj_kernel, d=d, scale=(d // FOX_HEADS) ** -0.5, normed=normed),
        grid=(b, s // tm),
        in_specs=[
            tok(d),
            pl.BlockSpec((None, None, N_MOD, d), lambda bi, si: (layer, bi, 0, 0)),
            pl.BlockSpec((None, 1, d), lambda bi, si: (layer, 0, 0)),
            _resident(w.shape),
            _resident(b_f.shape),
        ],
        out_specs=[tr(d), tok(d), tr(d), tr(LANES), tok(LANES)],
        out_shape=[out_t(d), out_bf16(d), out_t(d), out_t(LANES), out_bf16(LANES)],
        scratch_shapes=[pltpu.VMEM((1, LANES), F32)],
        compiler_params=_params(2),
        name="fox_proj",
    )(x, mod, g, w, b_f)


def _fox_attn_kernel(qt_ref, qxt_ref, k_ref, kx_ref, vt_ref, o_ref, s_ref, cmax_ref, p_ref):
    tq = ATTN_TILE
    hd = LANES // 2
    n_blocks = k_ref.shape[0] // tq
    n_slots = s_ref.shape[0]
    chan = lax.broadcasted_iota(jnp.int32, (2 * LANES, 1), 0)
    key = lax.broadcasted_iota(jnp.int32, (tq, tq), 0)
    qry = lax.broadcasted_iota(jnp.int32, (tq, tq), 1)
    causal = key <= qry
    ones = jnp.ones((SUM_ROWS, tq), BF16)
    keep = {}
    for pr in range(ATTN_PAIRS):
        for hh in range(2):
            head = 2 * (pl.program_id(1) * ATTN_PAIRS + pr) + hh
            bias_lo = LANES + BIAS_LANES * head
            keep[(pr, hh)] = (((chan >= hh * hd) & (chan < (hh + 1) * hd))
                              | ((chan >= bias_lo) & (chan < bias_lo + BIAS_LANES)))

    items = [(pr, n, hh, j) for pr in range(ATTN_PAIRS) for n in range(n_blocks)
             for j in [n] + list(range(n)) for hh in range(2)]

    def blk(i):
        return slice(i * tq, (i + 1) * tq)

    def cols(pr):
        return slice(pr * LANES, (pr + 1) * LANES)

    head_q = {}

    def scores(t):
        pr, n, hh, j = items[t]
        if (pr, n, hh) not in head_q:
            q_full = jnp.concatenate([qt_ref[cols(pr), blk(n)], qxt_ref[:, blk(n)]], axis=0)
            head_q[(pr, n, hh)] = jnp.where(keep[(pr, hh)], q_full, jnp.zeros_like(q_full))
        kb = jnp.concatenate([k_ref[blk(j), cols(pr)], kx_ref[blk(j), :]], axis=1)
        s = _dot(kb, head_q[(pr, n, hh)])
        if j == n:
            s = jnp.where(causal, s, -jnp.inf)
        s_ref[t % n_slots] = s
        cmax_ref[t % n_slots] = jnp.max(s, axis=0, keepdims=True)

    run_max = {}
    rescale = {}

    def weights(t):
        grp = items[t][:3]
        m_old = run_max.get(grp)
        m_new = cmax_ref[t % n_slots] if m_old is None else jnp.maximum(m_old, cmax_ref[t % n_slots])
        p_ref[t % n_slots] = jnp.exp2(s_ref[t % n_slots] - m_new).astype(BF16)
        rescale[t] = None if m_old is None else jnp.exp2(m_old - m_new)
        run_max[grp] = m_new

    assert n_slots > PIPE_DEPTH + 1
    for t in range(PIPE_DEPTH):
        scores(t)
    weights(0)
    acc = {}
    for t, (pr, n, hh, j) in enumerate(items):
        if t + PIPE_DEPTH < len(items):
            scores(t + PIPE_DEPTH)
        if t + 1 < len(items):
            weights(t + 1)
        rows = slice(pr * LANES + hh * hd, pr * LANES + (hh + 1) * hd)
        vt = jnp.concatenate([vt_ref[rows, blk(j)], ones], axis=0)
        pv = _dot(vt, p_ref[t % n_slots])
        acc[(pr, n, hh)] = pv if rescale[t] is None else rescale[t] * acc[(pr, n, hh)] + pv
        if j == (n - 1 if n else 0) and hh == 1:
            o_t = [acc[(pr, n, h)][:hd, :] / acc[(pr, n, h)][hd:hd + 1, :] for h in range(2)]
            o_ref[blk(n), cols(pr)] = jnp.concatenate(o_t, axis=0).T.astype(BF16)


def _fox_attn(qt, qxt, k, kx, vt):
    b, s, d = k.shape
    tq = ATTN_TILE
    n_slots = (s // tq) * (s // tq + 1)
    width = ATTN_PAIRS * LANES
    seq_blk = lambda w, col: pl.BlockSpec((None, s, w), col)
    chan_blk = lambda w, col: pl.BlockSpec((None, w, s), col)
    return pl.pallas_call(
        _fox_attn_kernel,
        grid=(b, d // width),
        in_specs=[
            chan_blk(width, lambda bi, p: (bi, p, 0)),
            chan_blk(LANES, lambda bi, p: (bi, 0, 0)),
            seq_blk(width, lambda bi, p: (bi, 0, p)),
            seq_blk(LANES, lambda bi, p: (bi, 0, 0)),
            chan_blk(width, lambda bi, p: (bi, p, 0)),
        ],
        out_specs=seq_blk(width, lambda bi, p: (bi, 0, p)),
        out_shape=jax.ShapeDtypeStruct((b, s, d), BF16),
        scratch_shapes=[pltpu.VMEM((n_slots, tq, tq), F32), pltpu.VMEM((n_slots, 1, tq), F32),
                        pltpu.VMEM((n_slots, tq, tq), BF16)],
        compiler_params=_params(2),
        name="fox_attn",
    )(qt, qxt, k, kx, vt)


def _gla_proj_kernel(x_ref, mod_ref, g_ref, w_ref, wa_ref, wa2_ref, ba_ref, go_ref,
                     qin_ref, kin_ref, kdec_ref, v_ref, gate_ref, dec_ref, b_ref,
                     *, dk, dv, scale, normed):
    tm = x_ref.shape[0]
    h = _input_h(x_ref, mod_ref, g_ref, normed)
    n_chunks = tm // CHUNK
    slab = SLAB_COLS
    n_slabs = 2 * dv // slab
    every = n_chunks // n_slabs

    def vr_slab(c):
        vr = _dot(h, w_ref[:, 2 * dk + c * slab:2 * dk + (c + 1) * slab])
        if (c + 1) * slab <= dv:
            v_ref[:, c * slab:(c + 1) * slab] = vr.astype(BF16)
        else:
            cols = slice(c * slab - dv, (c + 1) * slab - dv)
            gate_ref[:, cols] = _silu(vr) * go_ref[:, cols]

    a_lr = _dot(h, wa_ref[...]).astype(BF16)
    qk = _dot(h, w_ref[:, :2 * dk])
    log_alpha = _log_sigmoid(_dot(a_lr, wa2_ref[...]) + ba_ref[...]) / GLA_TAU
    for c in range(SLABS_AHEAD):
        vr_slab(c)

    row = lax.broadcasted_iota(jnp.int32, (CUM_BLOCK, CUM_BLOCK), 0)
    col = lax.broadcasted_iota(jnp.int32, (CUM_BLOCK, CUM_BLOCK), 1)
    tril = jnp.where((row >= col) & (row // CHUNK == col // CHUNK), 1.0, 0.0).astype(BF16)
    terms = jnp.concatenate(_split_bf16(log_alpha, GLA_SPLIT), axis=1)
    for i in range(tm // CUM_BLOCK):
        rows = slice(i * CUM_BLOCK, (i + 1) * CUM_BLOCK)
        sums = _dot(tril, terms[rows, :])
        b_ref[rows, :] = sum(sums[:, t * dk:(t + 1) * dk] for t in range(GLA_SPLIT))

    for c in range(n_chunks):
        if c % every == 0 and c // every + SLABS_AHEAD < n_slabs:
            vr_slab(c // every + SLABS_AHEAD)
        rows = slice(c * CHUNK, (c + 1) * CHUNK)
        b = b_ref[rows, :]
        b_last = b[CHUNK - 1:, :]
        k = qk[rows, dk:2 * dk]
        qin_ref[rows, :] = (qk[rows, :dk] * scale * jnp.exp(b)).astype(BF16)
        kin_ref[rows, :] = (k * jnp.exp(-b)).astype(BF16)
        kdec_ref[rows, :] = (k * jnp.exp(b_last - b)).astype(BF16)
        dec_ref[c:c + 1, :] = jnp.exp(b_last)


def _gla_proj(x, mod, g, w_stack, j, wa, wa2, ba, g_o, layer, normed):
    b, s, d = x.shape
    tm = GLA_TILE
    dk = wa2.shape[1]
    dv = g_o.shape[1]
    tok = lambda width: pl.BlockSpec((None, tm, width), lambda bi, si: (bi, si, 0))
    out = lambda width, dt: jax.ShapeDtypeStruct((b, s, width), dt)
    return pl.pallas_call(
        functools.partial(_gla_proj_kernel, dk=dk, dv=dv, scale=(dk // GLA_HEADS) ** -0.5,
                          normed=normed),
        grid=(b, s // tm),
        in_specs=[
            tok(d),
            pl.BlockSpec((None, None, N_MOD, d), lambda bi, si: (layer, bi, 0, 0)),
            pl.BlockSpec((None, 1, d), lambda bi, si: (layer, 0, 0)),
            _layer_of(w_stack, j),
            _resident(wa.shape),
            _resident(wa2.shape),
            _resident(ba.shape),
            _resident(g_o.shape),
        ],
        out_specs=[tok(dk), tok(dk), tok(dk), tok(dv), tok(dv),
                   pl.BlockSpec((None, tm // CHUNK, dk), lambda bi, si: (bi, si, 0))],
        out_shape=[out(dk, BF16), out(dk, BF16), out(dk, BF16), out(dv, BF16), out(dv, F32),
                   jax.ShapeDtypeStruct((b, s // CHUNK, dk), F32)],
        scratch_shapes=[pltpu.VMEM((tm, dk), F32)],
        compiler_params=_params(2),
        name="gla_proj",
    )(x, mod, g, w_stack, wa, wa2, ba, g_o)


def _gla_core_kernel(qin_ref, kin_ref, kdec_ref, v_ref, gate_ref, dec_ref,
                     o_ref, state_ref, kv_ref, a_ref):
    n_seq, tm, dk = qin_ref.shape
    hk = dk // GLA_HEADS
    hv = v_ref.shape[2] // GLA_HEADS

    @pl.when(pl.program_id(1) == 0)
    def _():
        state_ref[...] = jnp.zeros_like(state_ref)

    row = lax.broadcasted_iota(jnp.int32, (CHUNK, CHUNK), 0)
    col = lax.broadcasted_iota(jnp.int32, (CHUNK, CHUNK), 1)
    causal = row >= col

    items = [(c, hd, sq) for c in range(tm // CHUNK) for hd in range(GLA_HEADS)
             for sq in range(n_seq)]

    def span(c, hd):
        return (slice(c * CHUNK, (c + 1) * CHUNK), slice(hd * hk, (hd + 1) * hk),
                slice(hd * hv, (hd + 1) * hv))

    for i, (c, hd, sq) in enumerate(items):
        rows, kc, vc = span(c, hd)
        kv_ref[i] = _dot_tn(v_ref[sq, rows, vc], kdec_ref[sq, rows, kc])
        a = _dot_nt(qin_ref[sq, rows, kc], kin_ref[sq, rows, kc])
        a_ref[i] = jnp.where(causal, a, 0.0).astype(BF16)

    for i, (c, hd, sq) in enumerate(items):
        rows, kc, vc = span(c, hd)
        q = qin_ref[sq, rows, kc]
        state_t = state_ref[sq, hd]
        o = _dot(a_ref[i], v_ref[sq, rows, vc]) + _dot_nt(q, state_t.astype(BF16))
        state_ref[sq, hd] = state_t * dec_ref[sq, c:c + 1, kc] + kv_ref[i]
        o = o * lax.rsqrt(jnp.mean(o * o, axis=-1, keepdims=True) + EPS)
        o_ref[sq, rows, vc] = (o * gate_ref[sq, rows, vc]).astype(BF16)


def _gla_core(qin, kin, kdec, v, gate, dec):
    b, s, dk = qin.shape
    dv = v.shape[2]
    tm = TOKEN_TILE
    nb = GLA_BATCH
    n_items = nb * GLA_HEADS * (tm // CHUNK)
    tok = lambda width: pl.BlockSpec((nb, tm, width), lambda bi, si: (bi, si, 0))
    return pl.pallas_call(
        _gla_core_kernel,
        grid=(b // nb, s // tm),
        in_specs=[tok(dk), tok(dk), tok(dk), tok(dv), tok(dv),
                  pl.BlockSpec((nb, tm // CHUNK, dk), lambda bi, si: (bi, si, 0))],
        out_specs=tok(dv),
        out_shape=jax.ShapeDtypeStruct((b, s, dv), BF16),
        scratch_shapes=[pltpu.VMEM((nb, GLA_HEADS, dv // GLA_HEADS, dk // GLA_HEADS), F32),
                        pltpu.VMEM((n_items, dv // GLA_HEADS, dk // GLA_HEADS), F32),
                        pltpu.VMEM((n_items, CHUNK, CHUNK), BF16)],
        compiler_params=_params(2),
        name="gla_core",
    )(qin, kin, kdec, v, gate, dec)


def _mix_ffn_kernel(x_ref, o_ref, mod_ref, g_ref, wo_ref, win_ref, wout_ref, post_g_ref, post_mod_ref,
                    out_ref, *rest, hidden, final):
    acc_ref = rest[-1]
    tm = x_ref.shape[0]
    n_chunks = hidden // FFN_CHUNK
    parts = [slice(r * tm // ROW_PARTS, (r + 1) * tm // ROW_PARTS) for r in range(ROW_PARTS)]

    for rows in parts:
        out_ref[rows, :] = x_ref[rows, :] + mod_ref[2:3, :] * _dot(o_ref[rows, :], wo_ref[...])
    h = [_rms_mod(out_ref[rows, :], g_ref[...], mod_ref[3:4, :], mod_ref[4:5, :]).astype(BF16)
         for rows in parts]

    def gate_up(t):
        r, j = divmod(t, n_chunks)
        cols = slice(j * FFN_CHUNK, (j + 1) * FFN_CHUNK)
        up_cols = slice(hidden + j * FFN_CHUNK, hidden + (j + 1) * FFN_CHUNK)
        return (_silu(_dot(h[r], win_ref[:, cols])) * _dot(h[r], win_ref[:, up_cols])).astype(BF16)

    def epilogue(rows):
        y = out_ref[rows, :] + mod_ref[5:6, :] * acc_ref[rows, :]
        if final:
            y = y * lax.rsqrt(jnp.mean(y * y, axis=-1, keepdims=True) + EPS) * post_g_ref[...]
        else:
            rest[0][rows, :] = _rms_mod(
                y, post_g_ref[...], post_mod_ref[0:1, :], post_mod_ref[1:2, :]).astype(BF16)
        out_ref[rows, :] = y

    act = gate_up(0)
    for t in range(ROW_PARTS * n_chunks):
        nxt = gate_up(t + 1) if t + 1 < ROW_PARTS * n_chunks else None
        r, j = divmod(t, n_chunks)
        down = _dot(act, wout_ref[j * FFN_CHUNK:(j + 1) * FFN_CHUNK, :])
        if j == 0:
            acc_ref[parts[r], :] = down
        else:
            acc_ref[parts[r], :] += down
        if j == n_chunks - 1:
            epilogue(parts[r])
        act = nxt


def _mix_ffn(x, o, mod, g, wo_stack, j, win_stack, wout_stack, post_g, layer, final):
    b, s, d = x.shape
    tm = TOKEN_TILE
    tok = pl.BlockSpec((None, tm, d), lambda bi, si: (bi, si, 0))
    nxt = layer if final else layer + 1
    post_g_spec = (_resident(post_g.shape) if final
                   else pl.BlockSpec((None, 1, d), lambda bi, si: (nxt, 0, 0)))
    outs = [jax.ShapeDtypeStruct((b, s, d), F32)] + ([] if final else [jax.ShapeDtypeStruct((b, s, d), BF16)])
    res = pl.pallas_call(
        functools.partial(_mix_ffn_kernel, hidden=wout_stack.shape[1], final=final),
        grid=(b, s // tm),
        in_specs=[
            tok, tok,
            pl.BlockSpec((None, None, N_MOD, d), lambda bi, si: (layer, bi, 0, 0)),
            pl.BlockSpec((None, 1, d), lambda bi, si: (layer, 0, 0)),
            _layer_of(wo_stack, j), _layer_of(win_stack, layer), _layer_of(wout_stack, layer),
            post_g_spec,
            pl.BlockSpec((None, None, N_MOD, d), lambda bi, si: (nxt, bi, 0, 0)),
        ],
        out_specs=[tok] * len(outs),
        out_shape=outs,
        scratch_shapes=[pltpu.VMEM((tm, d), F32)],
        compiler_params=_params(2),
        name="mix_ffn",
    )(x, o, mod, g, wo_stack, win_stack, wout_stack, post_g, mod)
    return (res[0], None) if final else (res[0], res[1])


def kernel(x, c, ada_w, ada_b, norm1_g, norm2_g, ffn_w_in, ffn_w_out, fox_w_in, fox_b_f, fox_w_out,
           gla_w_in, gla_w_a2, gla_b_a, gla_g_o, gla_w_out, final_g):
    depth, d, _ = ada_w.shape
    b = x.shape[0]
    assert x.shape[1] % FOX_TILE == 0 and x.shape[1] % GLA_TILE == 0 and TOKEN_TILE % ATTN_TILE == 0 and FOX_TILE % PART_ROWS == 0
    assert ffn_w_out.shape[1] % FFN_CHUNK == 0
    mod = _adaln_mod(c, ada_w, ada_b).reshape(depth, b, N_MOD, d)
    g1 = norm1_g.reshape(depth, 1, d)
    g2 = norm2_g.reshape(depth, 1, d)
    fg = final_g.reshape(1, d)
    ffn_in, ffn_out = ffn_w_in.astype(BF16), ffn_w_out.astype(BF16)
    fox_in, fox_out = fox_w_in.astype(BF16), fox_w_out.astype(BF16)
    gla_in, gla_out = gla_w_in.astype(BF16), gla_w_out.astype(BF16)
    rank = gla_w_a2.shape[1]
    h = None
    for i in range(depth):
        j = i // 2
        src, normed = (x, False) if h is None else (h, True)
        if i % 2 == 0:
            w_f = fox_in[j][:, 3 * d:]
            pad = jnp.zeros((d, LANES - N_SPLIT * FOX_HEADS), BF16)
            w = jnp.concatenate([fox_in[j][:, :3 * d]] + [w_f] * N_SPLIT + [pad], axis=1)
            b_f = jnp.concatenate(
                [fox_b_f[j]] * N_SPLIT + [jnp.zeros((LANES - N_SPLIT * FOX_HEADS,), F32)])
            qt, k, vt, qxt, kx = _fox_proj(src, mod, g1, w, b_f.reshape(1, LANES), i, normed)
            o = _fox_attn(qt, qxt, k, kx, vt)
            w_o = fox_out
        else:
            n_main = gla_in.shape[2] - rank
            wa = jnp.pad(gla_in[j][:, n_main:], ((0, 0), (0, LANES - rank)))
            wa2 = jnp.pad(gla_w_a2[j], ((0, LANES - rank), (0, 0))).astype(BF16)
            qin, kin, kdec, v, gate, dec = _gla_proj(
                src, mod, g1, gla_in, j, wa, wa2, gla_b_a[j].reshape(1, -1),
                gla_g_o[j].reshape(1, -1), i, normed)
            o = _gla_core(qin, kin, kdec, v, gate, dec)
            w_o = gla_out
        final = i == depth - 1
        x, h = _mix_ffn(x, o, mod, g2, w_o, j, ffn_in, ffn_out, fg if final else g1, i, final)
    return x
```
